```python
import math
import jax, jax.numpy as jnp
from jax import lax
import numpy as np

D_MODEL = 1024
BATCH = 32
SEQ = 256
DEPTH = 2
DEC_BATCH = 8
DEC_SEQ = 4096
PAST_LEN = 512

GRID_W = 64
Q_BLOCK = 128
EPS = 1e-6
N_MOD = 6
N_EVEN = (DEPTH + 1) // 2
N_ODD = DEPTH // 2
A_HEADS = 4
A_QK_DIM = 64
A_V_DIM = 2 * A_QK_DIM
A_WIDTH = A_HEADS * A_V_DIM
B_CHANNELS = D_MODEL - A_WIDTH
CONV_WIDTH = 31
EVEN_IN = 3 * A_WIDTH + 2 * B_CHANNELS
ROPE_BASE = 10000.0
ROPE_FREQS = A_QK_DIM // 4
C_HEADS = 16
C_HEAD_DIM = D_MODEL // C_HEADS
NA_ROWS = 8
NA_COLS = 16
N_EXPERTS = 16
N_GROUPS = 4
EXPERTS_PER_GROUP = N_EXPERTS // N_GROUPS
TOP_K = 2
D_FF = 1024

kernel_name = 'hybrid_diffusion_prefix_step'


def _rms_norm(x, g):
    xf = x.astype(jnp.float32)
    y = xf * lax.rsqrt(jnp.mean(xf * xf, axis=-1, keepdims=True) + EPS)
    return (y * g.astype(jnp.float32)).astype(x.dtype)


def _layer_norm(x, g, b):
    xf = x.astype(jnp.float32)
    mu = jnp.mean(xf, axis=-1, keepdims=True)
    var = jnp.mean(jnp.square(xf - mu), axis=-1, keepdims=True)
    y = (xf - mu) * lax.rsqrt(var + EPS) * g.astype(jnp.float32) + b.astype(jnp.float32)
    return y.astype(x.dtype)


def _adaln(cond, w, b):
    mod = jax.nn.silu(cond) @ w + b
    return jnp.split(mod, N_MOD, axis=-1)


def _modulate(h, shift, scale):
    return h * (1 + scale[:, None, :]) + shift[:, None, :]


def _axial_rope(n_tokens):
    t = jnp.arange(n_tokens, dtype=jnp.int32)
    pos = jnp.stack([t // GRID_W, t % GRID_W], axis=-1).astype(jnp.float32)
    freqs = ROPE_BASE ** (-jnp.arange(ROPE_FREQS, dtype=jnp.float32) / ROPE_FREQS)
    ang = pos[:, :, None] * freqs
    return jnp.cos(ang), jnp.sin(ang)


def _apply_rope(x, cos, sin):
    b, n, g, _ = x.shape
    xr = x.reshape(b, n, g, 2, 2, ROPE_FREQS).astype(jnp.float32)
    x1, x2 = xr[..., 0, :], xr[..., 1, :]
    c = cos[None, :, None]
    s = sin[None, :, None]
    out = jnp.stack([x1 * c - x2 * s, x1 * s + x2 * c], axis=-2)
    return out.reshape(x.shape).astype(x.dtype)


def _sweep_queries(fn, qs):
    b, n = qs[0].shape[:2]
    nb = n // Q_BLOCK
    blocks = tuple(jnp.swapaxes(q.reshape((b, nb, Q_BLOCK) + q.shape[2:]), 0, 1) for q in qs)
    out = lax.map(lambda qb: fn(*qb), blocks)
    out = jnp.swapaxes(out, 0, 1)
    return out.reshape((b, n) + out.shape[3:])


def _diff_attention(q1, q2, k1, k2, v, lam):
    scale = A_QK_DIM ** -0.5

    def block(q1b, q2b):
        p1 = jax.nn.softmax(jnp.einsum('bqhd,bkhd->bhqk', q1b, k1).astype(jnp.float32) * scale, axis=-1)
        p2 = jax.nn.softmax(jnp.einsum('bqhd,bkhd->bhqk', q2b, k2).astype(jnp.float32) * scale, axis=-1)
        a = (p1 - lam * p2).astype(v.dtype)
        return jnp.einsum('bhqk,bkhd->bqhd', a, v)

    return _sweep_queries(block, (q1, q2))


def _softmax_attention(q, k, v):
    scale = q.shape[-1] ** -0.5

    def block(qb):
        p = jax.nn.softmax(jnp.einsum('bqhd,bkhd->bhqk', qb, k).astype(jnp.float32) * scale, axis=-1)
        return jnp.einsum('bhqk,bkhd->bqhd', p.astype(v.dtype), v)

    return _sweep_queries(block, (q,))


def _neighbourhood_attention(q, k, v, k_ctx, v_ctx, rpb):
    b, n, h, dh = q.shape
    rows = n // GRID_W
    wr = min(NA_ROWS, rows)
    scale = dh ** -0.5
    qg = q.reshape(b, rows, GRID_W, h, dh)
    kg = k.reshape(b, rows, GRID_W, h, dh)
    vg = v.reshape(b, rows, GRID_W, h, dh)
    r = jnp.arange(rows)
    row_start = jnp.clip(r - wr // 2, 0, rows - wr)
    row_off = row_start[:, None] + jnp.arange(wr)[None, :] - r[:, None] + (NA_ROWS - 1)
    cidx = jnp.arange(GRID_W)
    col_start = jnp.clip(cidx - NA_COLS // 2, 0, GRID_W - NA_COLS)
    col_keys = col_start[:, None] + jnp.arange(NA_COLS)[None, :]
    col_off = col_keys - cidx[:, None] + (NA_COLS - 1)
    rpb_cols = rpb.astype(jnp.float32)[:, :, col_off]
    n_loc = wr * NA_COLS

    def one_row(args):
        q_row, start, roff = args
        k_win = lax.dynamic_slice_in_dim(kg, start, wr, axis=1)[:, :, col_keys]
        v_win = lax.dynamic_slice_in_dim(vg, start, wr, axis=1)[:, :, col_keys]
        bias = jnp.transpose(rpb_cols[:, roff], (0, 2, 1, 3))
        s_loc = jnp.einsum('bqhd,bwqkhd->bhqwk', q_row, k_win).astype(jnp.float32) * scale + bias[None]
        s_ctx = jnp.einsum('bqhd,bkhd->bhqk', q_row, k_ctx).astype(jnp.float32) * scale
        s = jnp.concatenate([s_loc.reshape(b, h, GRID_W, n_loc), s_ctx], axis=-1)
        p = jax.nn.softmax(s, axis=-1).astype(v.dtype)
        p_loc = p[..., :n_loc].reshape(b, h, GRID_W, wr, NA_COLS)
        p_ctx = p[..., n_loc:]
        return (jnp.einsum('bhqwk,bwqkhd->bqhd', p_loc, v_win)
                + jnp.einsum('bhqk,bkhd->bqhd', p_ctx, v_ctx))

    out = lax.map(one_row, (jnp.moveaxis(qg, 1, 0), row_start, row_off))
    return jnp.moveaxis(out, 0, 1).reshape(b, n, h, dh)


def _depthwise_conv(u, w, bias):
    kernel = w[:, None, :].astype(u.dtype)
    y = lax.conv_general_dilated(u, kernel, (1,), [(CONV_WIDTH // 2, CONV_WIDTH // 2)],
                                 dimension_numbers=('NWC', 'WIO', 'NWC'),
                                 feature_group_count=u.shape[-1])
    return y + bias.astype(u.dtype)


def _even_mixer(h, w_in, w_out, lam_vecs, subln_g, conv_w, conv_b, cn_g, cn_b, lam_init, ctx_kv, rope):
    b, n, _ = h.shape
    proj = h @ w_in
    q, k, v, gv, gg = jnp.split(proj, [A_WIDTH, 2 * A_WIDTH, 3 * A_WIDTH, 3 * A_WIDTH + B_CHANNELS], axis=-1)
    q = q.reshape(b, n, 2 * A_HEADS, A_QK_DIM)
    k = k.reshape(b, n, 2 * A_HEADS, A_QK_DIM)
    if rope is not None:
        q = _apply_rope(q, rope[0], rope[1])
        k = _apply_rope(k, rope[0], rope[1])
    k = k.reshape(b, n, A_HEADS, A_V_DIM)
    v = v.reshape(b, n, A_HEADS, A_V_DIM)
    if ctx_kv is None:
        k_all, v_all = k, v
    else:
        k_all = jnp.concatenate([ctx_kv[0], k], axis=1)
        v_all = jnp.concatenate([ctx_kv[1], v], axis=1)
    kh = k_all.reshape(b, k_all.shape[1], A_HEADS, 2, A_QK_DIM)
    qh = q.reshape(b, n, A_HEADS, 2, A_QK_DIM)
    lf = lam_vecs.astype(jnp.float32)
    lam = jnp.exp(jnp.sum(lf[0] * lf[1])) - jnp.exp(jnp.sum(lf[2] * lf[3])) + lam_init
    attn = _diff_attention(qh[..., 0, :], qh[..., 1, :], kh[..., 0, :], kh[..., 1, :], v_all, lam)
    attn = (_rms_norm(attn, subln_g) * (1.0 - lam_init)).reshape(b, n, A_WIDTH)
    u = gv * jax.nn.sigmoid(gg)
    u = jax.nn.silu(_layer_norm(_depthwise_conv(u, conv_w, conv_b), cn_g, cn_b))
    out = jnp.concatenate([attn, u], axis=-1) @ w_out
    return out, (k, v)


def _odd_mixer(h, w_qkv, w_out, rpb, ctx_kv):
    b, n, _ = h.shape
    q, k, v = jnp.split(h @ w_qkv, 3, axis=-1)
    q = q.reshape(b, n, C_HEADS, C_HEAD_DIM)
    k = k.reshape(b, n, C_HEADS, C_HEAD_DIM)
    v = v.reshape(b, n, C_HEADS, C_HEAD_DIM)
    if ctx_kv is None:
        o = _softmax_attention(q, k, v)
    else:
        o = _neighbourhood_attention(q, k, v, ctx_kv[0], ctx_kv[1], rpb)
    return o.reshape(b, n, D_MODEL) @ w_out, (k, v)


def _moe(h, router_w, router_b, w1, w3, w2):
    b, n, d = h.shape
    x = h.reshape(b * n, d)
    scores = jax.nn.sigmoid((x @ router_w).astype(jnp.float32))
    sel = scores + router_b.astype(jnp.float32)
    group_score = lax.top_k(sel.reshape(-1, N_GROUPS, EXPERTS_PER_GROUP), TOP_K)[0].sum(-1)
    best = jnp.argmax(group_score, axis=-1)
    in_group = (jnp.arange(N_EXPERTS)[None, :] // EXPERTS_PER_GROUP) == best[:, None]
    _, idx = lax.top_k(jnp.where(in_group, sel, -jnp.inf), TOP_K)
    wsel = jnp.take_along_axis(scores, idx, axis=-1)
    wsel = wsel / jnp.sum(wsel, axis=-1, keepdims=True)
    gates = jnp.einsum('tk,tke->te', wsel, jax.nn.one_hot(idx, N_EXPERTS, dtype=jnp.float32)).astype(x.dtype)
    y = jnp.zeros_like(x)
    for e in range(N_EXPERTS):
        he = jax.nn.silu(x @ w1[e]) * (x @ w3[e])
        y = y + gates[:, e:e + 1] * (he @ w2[e])
    return y.reshape(b, n, d)


def setup_inputs(seed: int = 0) -> dict:
    key = jax.random.key(seed)
    ks = jax.random.split(key, 28)
    f32 = jnp.float32

    def nrm(k, shape, scale=1.0):
        return jax.random.normal(k, shape, f32) * scale

    inv = D_MODEL ** -0.5
    return {
        'x_prompt': nrm(ks[0], (BATCH, SEQ, D_MODEL)),
        'x_sample': nrm(ks[1], (DEC_BATCH, DEC_SEQ, D_MODEL)),
        'cache_even_k': nrm(ks[2], (DEC_BATCH, N_EVEN, PAST_LEN, A_HEADS, A_V_DIM)),
        'cache_even_v': nrm(ks[3], (DEC_BATCH, N_EVEN, PAST_LEN, A_HEADS, A_V_DIM)),
        'cache_odd_k': nrm(ks[4], (DEC_BATCH, N_ODD, PAST_LEN, C_HEADS, C_HEAD_DIM)),
        'cache_odd_v': nrm(ks[5], (DEC_BATCH, N_ODD, PAST_LEN, C_HEADS, C_HEAD_DIM)),
        'c': nrm(ks[6], (DEC_BATCH, D_MODEL)),
        'c_ctx': nrm(ks[7], (D_MODEL,)),
        'w_ada': nrm(ks[8], (DEPTH, D_MODEL, N_MOD * D_MODEL), 0.5 * inv),
        'b_ada': nrm(ks[9], (DEPTH, N_MOD * D_MODEL), 0.02),
        'norm_g': 1.0 + nrm(ks[10], (DEPTH, 2, D_MODEL), 0.02),
        'final_g': 1.0 + nrm(ks[11], (D_MODEL,), 0.02),
        'even_w_in': nrm(ks[12], (N_EVEN, D_MODEL, EVEN_IN), inv),
        'even_w_out': nrm(ks[13], (N_EVEN, D_MODEL, D_MODEL), inv),
        'even_lam': nrm(ks[14], (N_EVEN, 4, A_QK_DIM), 0.1),
        'even_subln_g': 1.0 + nrm(ks[15], (N_EVEN, A_V_DIM), 0.02),
        'even_conv_w': nrm(ks[16], (N_EVEN, CONV_WIDTH, B_CHANNELS), CONV_WIDTH ** -0.5),
        'even_conv_b': nrm(ks[17], (N_EVEN, B_CHANNELS), 0.02),
        'even_conv_norm_g': 1.0 + nrm(ks[18], (N_EVEN, B_CHANNELS), 0.02),
        'even_conv_norm_b': nrm(ks[19], (N_EVEN, B_CHANNELS), 0.02),
        'odd_w_qkv': nrm(ks[20], (N_ODD, D_MODEL, 3 * D_MODEL), inv),
        'odd_w_out': nrm(ks[21], (N_ODD, D_MODEL, D_MODEL), inv),
        'odd_rpb': nrm(ks[22], (N_ODD, C_HEADS, 2 * NA_ROWS - 1, 2 * NA_COLS - 1), 0.5),
        'router_w': nrm(ks[23], (D_MODEL, N_EXPERTS), inv),
        'router_b': nrm(ks[24], (N_EXPERTS,), 0.01),
        'moe_w1': nrm(ks[25], (DEPTH, N_EXPERTS, D_MODEL, D_FF), inv),
        'moe_w3': nrm(ks[26], (DEPTH, N_EXPERTS, D_MODEL, D_FF), inv),
        'moe_w2': nrm(ks[27], (DEPTH, N_EXPERTS, D_FF, D_MODEL), D_FF ** -0.5),
    }


def reference(x_prompt, x_sample, cache_even_k, cache_even_v, cache_odd_k, cache_odd_v, c, c_ctx,
              w_ada, b_ada, norm_g, final_g, even_w_in, even_w_out, even_lam, even_subln_g,
              even_conv_w, even_conv_b, even_conv_norm_g, even_conv_norm_b, odd_w_qkv, odd_w_out,
              odd_rpb, router_w, router_b, moe_w1, moe_w3, moe_w2):

    def layer(l, x, cond, ctx_kv, rope):
        shift1, scale1, gate1, shift2, scale2, gate2 = _adaln(cond, w_ada[l], b_ada[l])
        h = _modulate(_rms_norm(x, norm_g[l, 0]), shift1, scale1)
        j = l // 2
        if l % 2 == 0:
            lam_init = 0.8 - 0.6 * math.exp(-0.3 * l)
            mix, kv = _even_mixer(h, even_w_in[j], even_w_out[j], even_lam[j], even_subln_g[j],
                                  even_conv_w[j], even_conv_b[j], even_conv_norm_g[j],
                                  even_conv_norm_b[j], lam_init, ctx_kv, rope)
        else:
            mix, kv = _odd_mixer(h, odd_w_qkv[j], odd_w_out[j], odd_rpb[j], ctx_kv)
        x = x + gate1[:, None, :] * mix
        h = _modulate(_rms_norm(x, norm_g[l, 1]), shift2, scale2)
        x = x + gate2[:, None, :] * _moe(h, router_w, router_b, moe_w1[l], moe_w3[l], moe_w2[l])
        return x, kv

    x = x_prompt
    cond_ctx = c_ctx[None, :]
    even_kv = []
    odd_kv = []
    for l in range(DEPTH):
        x, kv = layer(l, x, cond_ctx, None, None)
        if l % 2 == 0:
            even_kv.append(kv)
        else:
            odd_kv.append(kv)
    y_prompt = _rms_norm(x, final_g)
    new_even_k = jnp.stack([kv[0] for kv in even_kv], axis=1)
    new_even_v = jnp.stack([kv[1] for kv in even_kv], axis=1)
    new_odd_k = jnp.stack([kv[0] for kv in odd_kv], axis=1)
    new_odd_v = jnp.stack([kv[1] for kv in odd_kv], axis=1)

    x = x_sample
    rope = _axial_rope(x_sample.shape[1])
    for l in range(DEPTH):
        j = l // 2
        if l % 2 == 0:
            ctx_kv = (cache_even_k[:, j], cache_even_v[:, j])
        else:
            ctx_kv = (cache_odd_k[:, j], cache_odd_v[:, j])
        x, _ = layer(l, x, c, ctx_kv, rope)
    y_sample = _rms_norm(x, final_g)

    return (y_prompt, y_sample, new_even_k, new_even_v, new_odd_k, new_odd_v)
```

```python
import functools

import numpy as np
import jax
import jax.numpy as jnp
from jax import lax
from jax.experimental import pallas as pl
from jax.experimental.pallas import tpu as pltpu

F32 = jnp.float32
BF16 = jnp.bfloat16
I32 = jnp.int32

GRID_W = 64
EPS = 1e-6
N_MOD = 6
A_HEADS = 4
A_QK_DIM = 64
A_V_DIM = 128
A_WIDTH = A_HEADS * A_V_DIM
CONV_WIDTH = 31
ROPE_BASE = 10000.0
ROPE_FREQS = A_QK_DIM // 4
C_HEADS = 16
C_HEAD_DIM = 64
NA_ROWS = 8
NA_COLS = 16
N_EXPERTS = 16
N_GROUPS = 4
EXPERTS_PER_GROUP = N_EXPERTS // N_GROUPS
NEG = -1e30

LANE = 128
BF16_ROWS = 16
VMEM_LIMIT = 52 * 1024 * 1024

TM = 256
TD = 512
TF = 256
HALO = 16
NA_QROWS = 4
NA_KROWS = NA_QROWS + NA_ROWS
COND_ROWS = 16


def _nt(a, b):
    return lax.dot_general(a, b, (((1,), (1,)), ((), ())), preferred_element_type=F32)


def _mm(a, b):
    return jnp.dot(a, b, preferred_element_type=F32)


def _split_bf16(a):
    hi = a.astype(BF16)
    lo = (a - hi.astype(F32)).astype(BF16)
    return hi, lo


def _sigmoid(x):
    return 1.0 / (1.0 + jnp.exp(-x))


def _rms(x, g):
    return x * lax.rsqrt(jnp.mean(x * x, axis=-1, keepdims=True) + EPS) * g


def _ada_kernel(cond_ref, w_ref, b_ref, o_ref):
    c = cond_ref[...]
    a = c * _sigmoid(c)
    ah, al = _split_bf16(a)
    wh, wl = _split_bf16(w_ref[0])
    o_ref[0] = _mm(ah, wh) + _mm(ah, wl) + _mm(al, wh) + b_ref[0]


def _ada(cond, w_ada, b_ada):
    depth, d, n = w_ada.shape
    tn = 1536
    return pl.pallas_call(
        _ada_kernel,
        grid=(depth, n // tn),
        in_specs=[pl.BlockSpec((COND_ROWS, d), lambda l, j: (0, 0)),
                  pl.BlockSpec((1, d, tn), lambda l, j: (l, 0, j)),
                  pl.BlockSpec((1, 1, tn), lambda l, j: (l, 0, j))],
        out_specs=pl.BlockSpec((1, COND_ROWS, tn), lambda l, j: (l, 0, j)),
        out_shape=jax.ShapeDtypeStruct((depth, COND_ROWS, n), F32),
        compiler_params=pltpu.CompilerParams(dimension_semantics=("arbitrary", "arbitrary"),
                                             vmem_limit_bytes=VMEM_LIMIT),
        name="ada",
    )(cond, w_ada, b_ada.reshape(depth, 1, n))


def _load_x(x_refs, t, pt):
    if len(x_refs) == 1:
        return x_refs[0][...]
    return jnp.where(t < pt, x_refs[0][...], x_refs[1][...])


def _pre_even_kernel(*refs, n_x, pt):
    x_refs = refs[:n_x]
    (mod_ref, g_ref, w_ref, c_ref, s1_ref, s2_ref,
     q_ref, k_ref, v_ref, u_ref, kf_ref, vf_ref) = refs[n_x:]
    t = pl.program_id(0)
    x = _load_x(x_refs, t, pt)
    mod = mod_ref[0, 0]
    h = _rms(x, g_ref[...]) * (1.0 + mod[1:2]) + mod[0:1]
    proj = _mm(h.astype(BF16), w_ref[...])
    aw = A_WIDTH
    q = proj[:, :aw]
    k = proj[:, aw:2 * aw]
    v = proj[:, 2 * aw:3 * aw]
    bw = (proj.shape[1] - 3 * aw) // 2
    gv = proj[:, 3 * aw:3 * aw + bw]
    gg = proj[:, 3 * aw + bw:]
    v_ref[...] = v.astype(BF16)
    u_ref[...] = (gv * _sigmoid(gg)).astype(BF16)
    scale = A_QK_DIM ** -0.5

    @pl.when(t < pt)
    def _():
        q_ref[...] = (q * scale).astype(BF16)
        k_ref[...] = k.astype(BF16)
        kf_ref[...] = k
        vf_ref[...] = v

    @pl.when(t >= pt)
    def _():
        reps = aw // LANE
        c = jnp.concatenate([c_ref[...]] * reps, axis=1)
        s1 = jnp.concatenate([s1_ref[...]] * reps, axis=1)
        s2 = jnp.concatenate([s2_ref[...]] * reps, axis=1)

        def rope(z):
            return (z * c + pltpu.roll(z, ROPE_FREQS, 1) * s1
                    + pltpu.roll(z, aw - ROPE_FREQS, 1) * s2)

        q_ref[...] = (rope(q) * scale).astype(BF16)
        k_ref[...] = rope(k).astype(BF16)


def _pre_odd_kernel(*refs, n_x, pt):
    x_refs = refs[:n_x]
    mod_ref, g_ref, w_ref, q_ref, k_ref, v_ref, kf_ref, vf_ref = refs[n_x:]
    t = pl.program_id(0)
    x = _load_x(x_refs, t, pt)
    mod = mod_ref[0, 0]
    h = _rms(x, g_ref[...]) * (1.0 + mod[1:2]) + mod[0:1]
    proj = _mm(h.astype(BF16), w_ref[...])
    d = proj.shape[1] // 3
    q = proj[:, :d]
    k = proj[:, d:2 * d]
    v = proj[:, 2 * d:]
    q_ref[...] = (q * (C_HEAD_DIM ** -0.5)).astype(BF16)
    k_ref[...] = k.astype(BF16)
    v_ref[...] = v.astype(BF16)

    @pl.when(t < pt)
    def _():
        kf_ref[...] = k
        vf_ref[...] = v


def _x_specs(xs, pt, d):
    if len(xs) == 1:
        return [pl.BlockSpec((TM, d), lambda t: (t, 0))]
    return [pl.BlockSpec((TM, d), lambda t: (jnp.minimum(t, pt - 1), 0)),
            pl.BlockSpec((TM, d), lambda t: (jnp.maximum(t - pt, 0), 0))]


def _pre(xs, mod, g, w, rope, *, layer, even, p_tok, s_seq):
    d = xs[0].shape[1]
    t_tok = sum(x.shape[0] for x in xs) if len(xs) == 2 else xs[0].shape[0]
    nt = t_tok // TM
    pt = p_tok // TM
    st = s_seq // TM
    x_specs = _x_specs(xs, pt, d)
    mod_spec = pl.BlockSpec((1, 1, N_MOD, d),
                            lambda t: (layer, jnp.where(t < pt, 0, 1 + (t - pt) // st), 0, 0))
    g_spec = pl.BlockSpec((1, d), lambda t: (0, 0))
    w_spec = pl.BlockSpec(w.shape, lambda t: (0, 0))
    tok = lambda width: pl.BlockSpec((TM, width), lambda t: (t, 0))
    ptok = lambda width: pl.BlockSpec((TM, width), lambda t: (jnp.minimum(t, pt - 1), 0))
    params = pltpu.CompilerParams(dimension_semantics=("arbitrary",), vmem_limit_bytes=VMEM_LIMIT)
    if even:
        aw = A_WIDTH
        bw = (w.shape[1] - 3 * aw) // 2
        rspec = pl.BlockSpec((TM, LANE), lambda t: (jnp.maximum(t - pt, 0) % st, 0))
        return pl.pallas_call(
            functools.partial(_pre_even_kernel, n_x=len(xs), pt=pt),
            grid=(nt,),
            in_specs=x_specs + [mod_spec, g_spec, w_spec, rspec, rspec, rspec],
            out_specs=[tok(aw), tok(aw), tok(aw), tok(bw), ptok(aw), ptok(aw)],
            out_shape=[jax.ShapeDtypeStruct((t_tok, aw), BF16)] * 3
            + [jax.ShapeDtypeStruct((t_tok, bw), BF16)]
            + [jax.ShapeDtypeStruct((p_tok, aw), F32)] * 2,
            compiler_params=params, name="pre_even",
        )(*xs, mod, g, w, *rope)
    return pl.pallas_call(
        functools.partial(_pre_odd_kernel, n_x=len(xs), pt=pt),
        grid=(nt,),
        in_specs=x_specs + [mod_spec, g_spec, w_spec],
        out_specs=[tok(d), tok(d), tok(d), ptok(d), ptok(d)],
        out_shape=[jax.ShapeDtypeStruct((t_tok, d), BF16)] * 3
        + [jax.ShapeDtypeStruct((p_tok, d), F32)] * 2,
        compiler_params=params, name="pre_odd",
    )(*xs, mod, g, w)


def _rope_tables(n_tokens):
    t = jnp.arange(n_tokens, dtype=I32)
    pos = jnp.stack([t // GRID_W, t % GRID_W], axis=-1).astype(F32)
    freqs = ROPE_BASE ** (-jnp.arange(ROPE_FREQS, dtype=F32) / ROPE_FREQS)
    ang = pos[:, :, None] * freqs
    cos, sin = jnp.cos(ang), jnp.sin(ang)
    zero = jnp.zeros_like(sin)
    c = jnp.concatenate([cos, cos], axis=-1).reshape(n_tokens, A_QK_DIM)
    s1 = jnp.concatenate([zero, sin], axis=-1).reshape(n_tokens, A_QK_DIM)
    s2 = jnp.concatenate([-sin, zero], axis=-1).reshape(n_tokens, A_QK_DIM)
    rep = LANE // A_QK_DIM
    return tuple(jnp.tile(a, (1, rep)) for a in (c, s1, s2))


def _pair_attn_kernel(*refs, diff, nk, tq, lam_init):
    if diff:
        q_ref, k_ref, v_ref, lam_ref, sg_ref, o_ref, qs, m_s, l_s, acc = refs
    else:
        q_ref, k_ref, v_ref, o_ref, qs, m_s, l_s, acc = refs
    ki = pl.program_id(3)
    half = LANE // 2

    @pl.when(ki == 0)
    def _():
        q = q_ref[...]
        lane = lax.broadcasted_iota(I32, q.shape, 1)
        zero = jnp.zeros_like(q)
        qs[0:tq, :] = jnp.where(lane < half, q, zero)
        qs[tq:, :] = jnp.where(lane >= half, q, zero)
        m_s[...] = jnp.full(m_s.shape, NEG, F32)
        l_s[...] = jnp.zeros(l_s.shape, F32)
        acc[...] = jnp.zeros(acc.shape, F32)

    s = _nt(qs[...], k_ref[...])
    m_prev = m_s[...]
    m_new = jnp.maximum(m_prev, jnp.max(s, axis=-1, keepdims=True))
    alpha = jnp.exp(m_prev - m_new)
    p = jnp.exp(s - m_new)
    l_s[...] = alpha * l_s[...] + jnp.sum(p, axis=-1, keepdims=True)
    acc[...] = alpha * acc[...] + _mm(p.astype(BF16), v_ref[...])
    m_s[...] = m_new

    @pl.when(ki == nk - 1)
    def _():
        o = acc[...] / l_s[...]
        o1 = o[0:tq]
        o2 = o[tq:]
        if diff:
            lf = lam_ref[...]
            lam = (jnp.exp(jnp.sum(lf[0:1] * lf[1:2], axis=1, keepdims=True))
                   - jnp.exp(jnp.sum(lf[2:3] * lf[3:4], axis=1, keepdims=True)) + lam_init)
            dlt = o1 - lam * o2
            o_ref[...] = (_rms(dlt, sg_ref[...]) * (1.0 - lam_init)).astype(o_ref.dtype)
        else:
            lane = lax.broadcasted_iota(I32, o1.shape, 1)
            o_ref[...] = jnp.where(lane < half, o1, o2).astype(o_ref.dtype)


def _pair_attn(q, k, v, *, batch, n_pairs, nq_tok, nk_tok, q_row0, k_row0, tq, tk,
               diff=False, lam=None, subln=None, lam_init=0.0):
    nq = nq_tok // tq
    nk = nk_tok // tk
    qb0, kb0 = q_row0 // tq, k_row0 // tk
    in_specs = [pl.BlockSpec((tq, LANE), lambda b, h, qi, ki: (qb0 + b * nq + qi, h)),
                pl.BlockSpec((tk, LANE), lambda b, h, qi, ki: (kb0 + b * nk + ki, h)),
                pl.BlockSpec((tk, LANE), lambda b, h, qi, ki: (kb0 + b * nk + ki, h))]
    args = [q, k, v]
    if diff:
        in_specs += [pl.BlockSpec(lam.shape, lambda b, h, qi, ki: (0, 0)),
                     pl.BlockSpec(subln.shape, lambda b, h, qi, ki: (0, 0))]
        args += [lam, subln]
    return pl.pallas_call(
        functools.partial(_pair_attn_kernel, diff=diff, nk=nk, tq=tq, lam_init=lam_init),
        grid=(batch, n_pairs, nq, nk),
        in_specs=in_specs,
        out_specs=pl.BlockSpec((tq, LANE), lambda b, h, qi, ki: (b * nq + qi, h)),
        out_shape=jax.ShapeDtypeStruct((batch * nq_tok, n_pairs * LANE), BF16),
        scratch_shapes=[pltpu.VMEM((2 * tq, LANE), BF16), pltpu.VMEM((2 * tq, 1), F32),
                        pltpu.VMEM((2 * tq, 1), F32), pltpu.VMEM((2 * tq, LANE), F32)],
        compiler_params=pltpu.CompilerParams(
            dimension_semantics=("arbitrary",) * 4, vmem_limit_bytes=VMEM_LIMIT),
        name="diff_attn" if diff else "pair_attn",
    )(*args)


def _na_kernel(q_ref, k_ref, v_ref, kc_ref, vc_ref, bias_ref, o_ref, *, nrb, n_tok):
    rb = pl.program_id(2)
    tq = NA_QROWS * GRID_W
    tkw = NA_KROWS * GRID_W
    half = LANE // 2
    q = q_ref[...]
    lane = lax.broadcasted_iota(I32, q.shape, 1)
    zero = jnp.zeros_like(q)
    qs = jnp.concatenate([jnp.where(lane < half, q, zero), jnp.where(lane >= half, q, zero)], axis=0)
    start = pl.multiple_of(jnp.clip(rb * tq - tq, 0, n_tok - tkw), tq)
    case = jnp.where(rb == 0, 0, jnp.where(rb == nrb - 1, 2, 1))
    s_loc = _nt(qs, k_ref[pl.ds(start, tkw), :]) + bias_ref[0, case]
    s_ctx = _nt(qs, kc_ref[0])
    m = jnp.maximum(jnp.max(s_loc, axis=-1, keepdims=True), jnp.max(s_ctx, axis=-1, keepdims=True))
    p_loc = jnp.exp(s_loc - m)
    p_ctx = jnp.exp(s_ctx - m)
    l = jnp.sum(p_loc, axis=-1, keepdims=True) + jnp.sum(p_ctx, axis=-1, keepdims=True)
    o = (_mm(p_loc.astype(BF16), v_ref[pl.ds(start, tkw), :]) + _mm(p_ctx.astype(BF16), vc_ref[0])) / l
    o_ref[...] = jnp.where(lane < half, o[0:tq], o[tq:]).astype(o_ref.dtype)


def _na_bias(rpb, rows):
    tq = NA_QROWS * GRID_W
    tkw = NA_KROWS * GRID_W
    nrb = rows // NA_QROWS
    ql = np.arange(tq)
    kl = np.arange(tkw)
    i, c = ql // GRID_W, ql % GRID_W
    j, kc = kl // GRID_W, kl % GRID_W
    dr, dc, ok = [], [], []
    for rb in (0, 1, nrb - 1):
        r = NA_QROWS * rb + i
        start = np.clip(NA_QROWS * rb - NA_QROWS, 0, rows - NA_KROWS)
        kr = start + j
        rs = np.clip(r - NA_ROWS // 2, 0, rows - NA_ROWS)
        cs = np.clip(c - NA_COLS // 2, 0, GRID_W - NA_COLS)
        okr = (kr[None, :] >= rs[:, None]) & (kr[None, :] < rs[:, None] + NA_ROWS)
        okc = (kc[None, :] >= cs[:, None]) & (kc[None, :] < cs[:, None] + NA_COLS)
        ok.append(okr & okc)
        dr.append(np.clip(kr[None, :] - r[:, None] + NA_ROWS - 1, 0, 2 * NA_ROWS - 2))
        dc.append(np.clip(kc[None, :] - c[:, None] + NA_COLS - 1, 0, 2 * NA_COLS - 2))
    dr, dc, ok = np.stack(dr), np.stack(dc), np.stack(ok)
    tab = jnp.where(ok[None], rpb.astype(F32)[:, dr, dc], NEG)
    h = rpb.shape[0]
    tab = tab.reshape(h // 2, 2, 3, tq, tkw).transpose(0, 2, 1, 3, 4)
    return tab.reshape(h // 2, 3, 2 * tq, tkw)


def _na(q, k, v, kc, vc, bias, *, batch, n_tok, row0):
    d = q.shape[1]
    n_pairs = d // LANE
    tq = NA_QROWS * GRID_W
    nrb = n_tok // tq
    past = kc.shape[1]
    qb0 = row0 // tq
    kb0 = row0 // n_tok
    return pl.pallas_call(
        functools.partial(_na_kernel, nrb=nrb, n_tok=n_tok),
        grid=(n_pairs, batch, nrb),
        in_specs=[pl.BlockSpec((tq, LANE), lambda h, b, r: (qb0 + b * nrb + r, h)),
                  pl.BlockSpec((n_tok, LANE), lambda h, b, r: (kb0 + b, h)),
                  pl.BlockSpec((n_tok, LANE), lambda h, b, r: (kb0 + b, h)),
                  pl.BlockSpec((1, past, LANE), lambda h, b, r: (b, 0, h)),
                  pl.BlockSpec((1, past, LANE), lambda h, b, r: (b, 0, h)),
                  pl.BlockSpec((1,) + bias.shape[1:], lambda h, b, r: (h, 0, 0, 0))],
        out_specs=pl.BlockSpec((tq, LANE), lambda h, b, r: (b * nrb + r, h)),
        out_shape=jax.ShapeDtypeStruct((batch * n_tok, d), BF16),
        compiler_params=pltpu.CompilerParams(
            dimension_semantics=("arbitrary",) * 3, vmem_limit_bytes=VMEM_LIMIT),
        name="na_attn",
    )(q, k, v, kc, vc, bias)


def _route(h2, rwh_ref, rwl_ref, rb_ref, route_ref, cnt_ref):
    hh, hl = _split_bf16(h2)
    logits = _nt(rwh_ref[...], hh) + _nt(rwh_ref[...], hl) + _nt(rwl_ref[...], hh)
    scores = _sigmoid(logits)
    sel = scores + rb_ref[...]
    tm = sel.shape[1]
    gs = []
    for g in range(N_GROUPS):
        r = [sel[g * EXPERTS_PER_GROUP + a:g * EXPERTS_PER_GROUP + a + 1] for a in range(EXPERTS_PER_GROUP)]
        best = None
        for a in range(EXPERTS_PER_GROUP):
            for b in range(a + 1, EXPERTS_PER_GROUP):
                pair = r[a] + r[b]
                best = pair if best is None else jnp.maximum(best, pair)
        gs.append(best)
    bg = jnp.zeros((1, tm), I32)
    bs = gs[0]
    for g in range(1, N_GROUPS):
        better = gs[g] > bs
        bg = jnp.where(better, g, bg)
        bs = jnp.where(better, gs[g], bs)
    eidx = lax.broadcasted_iota(I32, sel.shape, 0)
    masked = jnp.where(eidx // EXPERTS_PER_GROUP == bg, sel, -jnp.inf)
    m1 = jnp.max(masked, axis=0, keepdims=True)
    i1 = jnp.min(jnp.where(masked == m1, eidx, N_EXPERTS), axis=0, keepdims=True)
    masked2 = jnp.where(eidx == i1, -jnp.inf, masked)
    m2 = jnp.max(masked2, axis=0, keepdims=True)
    i2 = jnp.min(jnp.where(masked2 == m2, eidx, N_EXPERTS), axis=0, keepdims=True)
    s1 = jnp.sum(jnp.where(eidx == i1, scores, 0.0), axis=0, keepdims=True)
    s2 = jnp.sum(jnp.where(eidx == i2, scores, 0.0), axis=0, keepdims=True)
    tot = s1 + s2
    route_ref[0] = jnp.concatenate(
        [i1.astype(F32), i2.astype(F32), s1 / tot, s2 / tot, jnp.zeros((4, tm), F32)], axis=0)
    chosen = jnp.where(eidx == i1, 1.0, jnp.where(eidx == i2, 1.0, 0.0))
    cnt_ref[0] = jnp.broadcast_to(jnp.sum(chosen, axis=1, keepdims=True), cnt_ref.shape[1:])


def _post_kernel(*refs, even, n_x, pt, st):
    if even:
        (a_ref, up_ref, uc_ref, un_ref, cw_ref, cb_ref, cg_ref, cnb_ref, wo_ref) = refs[:9]
        rest = refs[9:]
    else:
        a_ref, wo_ref = refs[:2]
        rest = refs[2:]
    x_refs = rest[:n_x]
    (mod_ref, g2_ref, rwh_ref, rwl_ref, rb_ref, xn_ref, h2_ref, route_ref, cnt_ref) = rest[n_x:n_x + 9]
    t = pl.program_id(0)
    if even:
        ext = rest[n_x + 9]
        js = jnp.maximum(t - pt, 0) % st
        has_prev = jnp.logical_and(t >= pt, js != 0)
        has_next = jnp.logical_and(t >= pt, js != st - 1)
        up = up_ref[...].astype(F32)
        un = un_ref[...].astype(F32)
        ext[0:HALO, :] = jnp.where(has_prev, up, jnp.zeros_like(up))
        ext[HALO:HALO + TM, :] = uc_ref[...].astype(F32)
        ext[HALO + TM:, :] = jnp.where(has_next, un, jnp.zeros_like(un))
        pad = CONV_WIDTH // 2
        acc = jnp.zeros((TM, ext.shape[1]), F32) + cb_ref[...]
        for j in range(CONV_WIDTH):
            acc = acc + cw_ref[j:j + 1, :] * ext[pl.ds(HALO - pad + j, TM), :]
        mu = jnp.mean(acc, axis=-1, keepdims=True)
        var = jnp.mean(jnp.square(acc - mu), axis=-1, keepdims=True)
        y = (acc - mu) * lax.rsqrt(var + EPS) * cg_ref[...] + cnb_ref[...]
        ua = (y * _sigmoid(y)).astype(BF16)
        aw = a_ref.shape[1]
        mix = _mm(a_ref[...], wo_ref[0:aw, :]) + _mm(ua, wo_ref[aw:, :])
    else:
        mix = _mm(a_ref[...], wo_ref[...])
    x = _load_x(x_refs, t, pt)
    mod = mod_ref[0, 0]
    xn = x + mod[2:3] * mix
    xn_ref[...] = xn
    h2 = _rms(xn, g2_ref[...]) * (1.0 + mod[4:5]) + mod[3:4]
    h2_ref[...] = h2.astype(BF16)
    _route(h2, rwh_ref, rwl_ref, rb_ref, route_ref, cnt_ref)


def _post(a, u, conv, wo, xs, mod, g2, rwh, rwl, rbias, *, layer, even, p_tok, s_seq):
    d = wo.shape[1]
    t_tok = a.shape[0]
    nt = t_tok // TM
    pt = p_tok // TM
    st = s_seq // TM
    tok = lambda width: pl.BlockSpec((TM, width), lambda t: (t, 0))
    full = lambda arr: pl.BlockSpec(arr.shape, lambda t: (0,) * arr.ndim)
    in_specs, args, scratch = [], [], []
    if even:
        bw = u.shape[1]
        hb = TM // HALO
        in_specs += [tok(a.shape[1]),
                     pl.BlockSpec((HALO, bw), lambda t: (jnp.maximum(t * hb - 1, 0), 0)),
                     tok(bw),
                     pl.BlockSpec((HALO, bw), lambda t: (jnp.minimum((t + 1) * hb, nt * hb - 1), 0))]
        args += [a, u, u, u]
        for arr in conv:
            in_specs.append(full(arr))
            args.append(arr)
        in_specs.append(full(wo))
        args.append(wo)
        scratch = [pltpu.VMEM((TM + 2 * HALO, bw), F32)]
    else:
        in_specs += [tok(a.shape[1]), full(wo)]
        args += [a, wo]
    in_specs += _x_specs(xs, pt, d)
    args += list(xs)
    in_specs += [pl.BlockSpec((1, 1, N_MOD, d),
                              lambda t: (layer, jnp.where(t < pt, 0, 1 + (t - pt) // st), 0, 0)),
                 full(g2), full(rwh), full(rwl), full(rbias)]
    args += [mod, g2, rwh, rwl, rbias]
    return pl.pallas_call(
        functools.partial(_post_kernel, even=even, n_x=len(xs), pt=pt, st=st),
        grid=(nt,),
        in_specs=in_specs,
        out_specs=[tok(d), tok(d),
                   pl.BlockSpec((1, 8, TM), lambda t: (t, 0, 0)),
                   pl.BlockSpec((1, N_EXPERTS, LANE), lambda t: (t, 0, 0))],
        out_shape=[jax.ShapeDtypeStruct((t_tok, d), F32), jax.ShapeDtypeStruct((t_tok, d), BF16),
                   jax.ShapeDtypeStruct((nt, 8, TM), F32),
                   jax.ShapeDtypeStruct((nt, N_EXPERTS, LANE), F32)],
        scratch_shapes=scratch,
        compiler_params=pltpu.CompilerParams(dimension_semantics=("arbitrary",),
                                             vmem_limit_bytes=VMEM_LIMIT),
        name="post_even" if even else "post_odd",
    )(*args)


def _stage_rows():
    return 2 * TD + N_EXPERTS * BF16_ROWS


def _chunk_bits():
    bits = []
    b = TD
    while b >= BF16_ROWS:
        bits.append(b)
        b //= 2
    return bits


def _moe_plan(cnt, t_tok):
    nd = t_tok // TD
    cnt = cnt[:, :, 0].astype(I32).reshape(nd, TD // TM, N_EXPERTS).sum(axis=1)
    cnt16 = (cnt + BF16_ROWS - 1) // BF16_ROWS * BF16_ROWS
    loff = jnp.cumsum(cnt16, axis=1) - cnt16
    tot = cnt16.sum(axis=0)
    reg = (tot + TF - 1) // TF * TF
    off = jnp.cumsum(reg) - reg
    goff = off[None, :] + jnp.cumsum(cnt16, axis=0) - cnt16
    ends = jnp.cumsum(reg // TF)
    n_used = ends[-1]
    n_tiles = _sorted_rows(t_tok) // TF
    tile = jnp.minimum(jnp.arange(n_tiles, dtype=I32), n_used - 1)
    tile_expert = jnp.minimum(jnp.searchsorted(ends, tile, side="right"), N_EXPERTS - 1).astype(I32)
    tail = jnp.stack([n_used * TF, n_tiles - n_used])
    return dict(cnt16=cnt16.reshape(-1), loff=loff.reshape(-1), goff=goff.reshape(-1),
                gap_start=jnp.concatenate([off + tot, tail[0:1]]).astype(I32),
                gap_len=jnp.concatenate([reg - tot, tail[1:2]]).astype(I32),
                tile=tile.astype(I32), tile_expert=tile_expert,
                n_used=jnp.reshape(n_used, (1,)).astype(I32))


def _sorted_rows(t_tok):
    nd = t_tok // TD
    worst = 2 * t_tok + nd * N_EXPERTS * (BF16_ROWS - 1) + N_EXPERTS * (TF - 1)
    return (worst + TF - 1) // TF * TF


def _route_rows(route_ref):
    parts = [route_ref[j] for j in range(route_ref.shape[0])]
    return parts[0] if len(parts) == 1 else jnp.concatenate(parts, axis=1)


def _perm(route, loff_ref, tile):
    e0 = route[0:1].astype(I32)
    e1 = route[1:2].astype(I32)
    eidx = lax.broadcasted_iota(I32, (N_EXPERTS, TD), 0)
    m0 = eidx == e0
    m1 = eidx == e1
    chosen = jnp.where(m0, 1.0, jnp.where(m1, 1.0, 0.0)).astype(BF16)
    before = (lax.broadcasted_iota(I32, (TD, TD), 0) < lax.broadcasted_iota(I32, (TD, TD), 1))
    rank = _mm(chosen, jnp.where(before, 1.0, 0.0).astype(BF16))
    ecol = lax.broadcasted_iota(I32, (N_EXPERTS, 1), 0)
    lcol = jnp.zeros((N_EXPERTS, 1), F32)
    for e in range(N_EXPERTS):
        lcol = jnp.where(ecol == e, loff_ref[tile * N_EXPERTS + e].astype(F32), lcol)
    base = rank + lcol
    slot0 = jnp.sum(jnp.where(m0, base, 0.0), axis=0, keepdims=True).astype(I32)
    slot1 = jnp.sum(jnp.where(m1, base, 0.0), axis=0, keepdims=True).astype(I32)
    rio = lax.broadcasted_iota(I32, (_stage_rows(), TD), 0)
    return rio == slot0, rio == slot1


def _chunk_copies(cnt_ref, loff_ref, goff_ref, tile, stage, sorted_hbm, sem, *, to_sorted, wait):
    for e in range(N_EXPERTS):
        n = cnt_ref[tile * N_EXPERTS + e]
        lo = loff_ref[tile * N_EXPERTS + e]
        go = goff_ref[tile * N_EXPERTS + e]
        for bit in _chunk_bits():
            done = n & ~(2 * bit - 1)
            s_view = stage.at[pl.ds(pl.multiple_of(lo + done, BF16_ROWS), bit), :]
            h_view = sorted_hbm.at[pl.ds(pl.multiple_of(go + done, BF16_ROWS), bit), :]
            cp = (pltpu.make_async_copy(s_view, h_view, sem) if to_sorted
                  else pltpu.make_async_copy(h_view, s_view, sem))

            @pl.when((n & bit) != 0)
            def _(cp=cp):
                cp.wait() if wait else cp.start()


def _dispatch_kernel(cnt_ref, loff_ref, goff_ref, gs_ref, gl_ref,
                     route_ref, h2_ref, xs_hbm, stage, zeros, sem, zsem):
    i = pl.program_id(0)
    n = pl.num_programs(0)
    slot = i % 2
    p0, p1 = _perm(_route_rows(route_ref), loff_ref, i)
    p = jnp.where(p0, 1.0, jnp.where(p1, 1.0, 0.0)).astype(BF16)
    stage[slot] = _mm(p, h2_ref[...]).astype(BF16)
    copies = functools.partial(_chunk_copies, cnt_ref, loff_ref, goff_ref, to_sorted=True)
    copies(i, stage.at[slot], xs_hbm, sem.at[slot], wait=False)

    @pl.when(i > 0)
    def _():
        copies(i - 1, stage.at[1 - slot], xs_hbm, sem.at[1 - slot], wait=True)

    @pl.when(i == n - 1)
    def _():
        zeros[...] = jnp.zeros(zeros.shape, zeros.dtype)
        bits = [b for b in _chunk_bits() if b < TF]
        for wait in (False, True):
            for e in range(N_EXPERTS):
                g0, gl = gs_ref[e], gl_ref[e]
                for bit in bits:
                    done = gl & ~(2 * bit - 1)
                    cp = pltpu.make_async_copy(
                        zeros.at[pl.ds(0, bit), :],
                        xs_hbm.at[pl.ds(pl.multiple_of(g0 + done, BF16_ROWS), bit), :], zsem)

                    @pl.when((gl & bit) != 0)
                    def _(cp=cp, wait=wait):
                        cp.wait() if wait else cp.start()
        tail0, ntail = gs_ref[N_EXPERTS], gl_ref[N_EXPERTS]

        def tail_copy(j):
            return pltpu.make_async_copy(
                zeros, xs_hbm.at[pl.ds(pl.multiple_of(tail0 + j * TF, TF), TF), :], zsem)

        @pl.loop(0, ntail)
        def _(j):
            tail_copy(j).start()

        @pl.loop(0, ntail)
        def _(j):
            tail_copy(j).wait()

        copies(i, stage.at[slot], xs_hbm, sem.at[slot], wait=True)


def _dispatch(plan, route, h2):
    t_tok, d = h2.shape
    nd = t_tok // TD
    k = TD // TM
    rows = _sorted_rows(t_tok)
    grid_spec = pltpu.PrefetchScalarGridSpec(
        num_scalar_prefetch=5,
        grid=(nd,),
        in_specs=[pl.BlockSpec((k, 8, TM), lambda i, *_: (i, 0, 0)),
                  pl.BlockSpec((TD, d), lambda i, *_: (i, 0))],
        out_specs=pl.BlockSpec(memory_space=pl.ANY),
        scratch_shapes=[pltpu.VMEM((2, _stage_rows(), d), BF16), pltpu.VMEM((TF, d), BF16),
                        pltpu.SemaphoreType.DMA((2,)), pltpu.SemaphoreType.DMA(())],
    )
    return pl.pallas_call(
        _dispatch_kernel,
        grid_spec=grid_spec,
        out_shape=jax.ShapeDtypeStruct((rows, d), BF16),
        compiler_params=pltpu.CompilerParams(dimension_semantics=("arbitrary",),
                                             vmem_limit_bytes=VMEM_LIMIT),
        name="moe_dispatch",
    )(plan["cnt16"], plan["loff"], plan["goff"], plan["gap_start"], plan["gap_len"], route, h2)


def _ffn_kernel(tile_ref, te_ref, nu_ref, x_ref, w1_ref, w3_ref, w2_ref, y_ref):
    i = pl.program_id(0)

    @pl.when(i < nu_ref[0])
    def _():
        x = x_ref[...]
        a = _mm(x, w1_ref[0])
        b = _mm(x, w3_ref[0])
        he = (a * _sigmoid(a) * b).astype(BF16)
        y_ref[...] = _mm(he, w2_ref[0]).astype(y_ref.dtype)

    @pl.when(i >= nu_ref[0])
    def _():
        y_ref[...] = jnp.zeros(y_ref.shape, y_ref.dtype)


def _ffn(plan, xs, w1, w3, w2):
    rows, d = xs.shape
    f = w1.shape[2]
    grid_spec = pltpu.PrefetchScalarGridSpec(
        num_scalar_prefetch=3,
        grid=(rows // TF,),
        in_specs=[pl.BlockSpec((TF, d), lambda i, tile, te, nu: (tile[i], 0)),
                  pl.BlockSpec((1, d, f), lambda i, tile, te, nu: (te[i], 0, 0)),
                  pl.BlockSpec((1, d, f), lambda i, tile, te, nu: (te[i], 0, 0)),
                  pl.BlockSpec((1, f, d), lambda i, tile, te, nu: (te[i], 0, 0))],
        out_specs=pl.BlockSpec((TF, d), lambda i, tile, te, nu: (i, 0)),
    )
    return pl.pallas_call(
        _ffn_kernel,
        grid_spec=grid_spec,
        out_shape=jax.ShapeDtypeStruct((rows, d), BF16),
        compiler_params=pltpu.CompilerParams(dimension_semantics=("arbitrary",),
                                             vmem_limit_bytes=VMEM_LIMIT),
        name="moe_ffn",
    )(plan["tile"], plan["tile_expert"], plan["n_used"], xs, w1, w3, w2)


def _combine_kernel(*refs, final, ptd):
    cnt_ref, loff_ref, goff_ref, route_ref, ys_hbm, xn_ref, mod_ref = refs[:7]
    if final:
        fg_ref, yp_ref, ysm_ref, stage, sem = refs[7:]
    else:
        xo_ref, stage, sem = refs[7:]
    i = pl.program_id(0)
    n = pl.num_programs(0)
    slot = i % 2
    copies = functools.partial(_chunk_copies, cnt_ref, loff_ref, goff_ref, to_sorted=False)

    @pl.when(i == 0)
    def _():
        stage[...] = jnp.zeros(stage.shape, stage.dtype)
        copies(i, stage.at[0], ys_hbm, sem.at[0], wait=False)

    @pl.when(i + 1 < n)
    def _():
        copies(i + 1, stage.at[1 - slot], ys_hbm, sem.at[1 - slot], wait=False)

    route = _route_rows(route_ref)
    p0, p1 = _perm(route, loff_ref, i)
    gate_rows = jnp.where(p0, route[2:3], jnp.where(p1, route[3:4], 0.0))
    gate_col = jnp.sum(gate_rows, axis=1, keepdims=True)
    p = jnp.where(p0, 1.0, jnp.where(p1, 1.0, 0.0)).astype(BF16)
    copies(i, stage.at[slot], ys_hbm, sem.at[slot], wait=True)
    scaled = (stage[slot].astype(F32) * gate_col).astype(BF16)
    y = lax.dot_general(p, scaled, (((0,), (0,)), ((), ())), preferred_element_type=F32)
    xo = xn_ref[...] + mod_ref[0, 0][5:6] * y
    if final:
        out = _rms(xo, fg_ref[...])

        @pl.when(i < ptd)
        def _():
            yp_ref[...] = out

        @pl.when(i >= ptd)
        def _():
            ysm_ref[...] = out
    else:
        xo_ref[...] = xo


def _combine(plan, route, ys, xn, mod, final_g, *, layer, final, p_tok, s_seq):
    t_tok, d = xn.shape
    nd = t_tok // TD
    k = TD // TM
    ptd = p_tok // TD
    std = s_seq // TD
    in_specs = [pl.BlockSpec((k, 8, TM), lambda i, *_: (i, 0, 0)),
                pl.BlockSpec(memory_space=pl.ANY),
                pl.BlockSpec((TD, d), lambda i, *_: (i, 0)),
                pl.BlockSpec((1, 1, N_MOD, d),
                             lambda i, *_: (layer, jnp.where(i < ptd, 0, 1 + (i - ptd) // std), 0, 0))]
    args = [route, ys, xn, mod]
    if final:
        in_specs.append(pl.BlockSpec((1, d), lambda i, *_: (0, 0)))
        args.append(final_g)
        out_specs = [pl.BlockSpec((TD, d), lambda i, *_: (jnp.minimum(i, ptd - 1), 0)),
                     pl.BlockSpec((TD, d), lambda i, *_: (jnp.maximum(i - ptd, 0), 0))]
        out_shape = [jax.ShapeDtypeStruct((p_tok, d), F32),
                     jax.ShapeDtypeStruct((t_tok - p_tok, d), F32)]
    else:
        out_specs = pl.BlockSpec((TD, d), lambda i, *_: (i, 0))
        out_shape = jax.ShapeDtypeStruct((t_tok, d), F32)
    grid_spec = pltpu.PrefetchScalarGridSpec(
        num_scalar_prefetch=3, grid=(nd,), in_specs=in_specs, out_specs=out_specs,
        scratch_shapes=[pltpu.VMEM((2, _stage_rows(), d), BF16), pltpu.SemaphoreType.DMA((2,))])
    return pl.pallas_call(
        functools.partial(_combine_kernel, final=final, ptd=ptd),
        grid_spec=grid_spec, out_shape=out_shape,
        compiler_params=pltpu.CompilerParams(dimension_semantics=("arbitrary",),
                                             vmem_limit_bytes=VMEM_LIMIT),
        name="moe_combine_final" if final else "moe_combine",
    )(plan["cnt16"], plan["loff"], plan["goff"], *args)


def kernel(x_prompt, x_sample, cache_even_k, cache_even_v, cache_odd_k, cache_odd_v, c, c_ctx, w_ada, b_ada, norm_g, final_g, even_w_in, even_w_out, even_lam, even_subln_g, even_conv_w, even_conv_b, even_conv_norm_g, even_conv_norm_b, odd_w_qkv, odd_w_out, odd_rpb, router_w, router_b, moe_w1, moe_w3, moe_w2):
    batch, seq, d = x_prompt.shape
    dbatch, dseq, _ = x_sample.shape
    past = cache_even_k.shape[2]
    depth = w_ada.shape[0]
    p_tok, s_tok = batch * seq, dbatch * dseq
    t_tok = p_tok + s_tok
    assert seq == TM and dseq % TD == 0 and p_tok % TD == 0 and p_tok % dseq == 0
    assert 1 + dbatch <= COND_ROWS and dseq % GRID_W == 0 and past % TM == 0
    assert dseq // GRID_W >= NA_KROWS + NA_QROWS

    cond = jnp.zeros((COND_ROWS, d), F32).at[0].set(c_ctx).at[1:1 + dbatch].set(c)
    mod = _ada(cond, w_ada, b_ada).reshape(depth, COND_ROWS, N_MOD, d)
    rope = _rope_tables(dseq)
    rwh, rwl = _split_bf16(router_w.T)
    rbias = router_b.reshape(N_EXPERTS, 1)
    xs = [x_prompt.reshape(p_tok, d), x_sample.reshape(s_tok, d)]
    outs = {}

    for layer in range(depth):
        j = layer // 2
        even = layer % 2 == 0
        g1 = norm_g[layer, 0].reshape(1, d)
        g2 = norm_g[layer, 1].reshape(1, d)
        if even:
            q, k, v, u, kf, vf = _pre(xs, mod, g1, even_w_in[j].astype(BF16), rope, layer=layer,
                                      even=True, p_tok=p_tok, s_seq=dseq)
            outs["even_k"] = kf.reshape(batch, 1, seq, A_HEADS, A_V_DIM)
            outs["even_v"] = vf.reshape(batch, 1, seq, A_HEADS, A_V_DIM)
            lam_init = 0.8 - 0.6 * float(np.exp(-0.3 * layer))
            lam = even_lam[j]
            subln = even_subln_g[j].reshape(1, A_V_DIM)
            ap = _pair_attn(q, k, v, batch=batch, n_pairs=A_HEADS, nq_tok=seq, nk_tok=seq, q_row0=0,
                            k_row0=0, tq=seq, tk=seq, diff=True, lam=lam, subln=subln, lam_init=lam_init)
            width = A_WIDTH
            k_all = jnp.concatenate([cache_even_k[:, j].reshape(dbatch, past, width).astype(BF16),
                                     k[p_tok:].reshape(dbatch, dseq, width)], axis=1)
            v_all = jnp.concatenate([cache_even_v[:, j].reshape(dbatch, past, width).astype(BF16),
                                     v[p_tok:].reshape(dbatch, dseq, width)], axis=1)
            nk_tok = past + dseq
            a_s = _pair_attn(q, k_all.reshape(dbatch * nk_tok, width), v_all.reshape(dbatch * nk_tok, width),
                             batch=dbatch, n_pairs=A_HEADS, nq_tok=dseq, nk_tok=nk_tok, q_row0=p_tok,
                             k_row0=0, tq=TM, tk=TD, diff=True, lam=lam, subln=subln, lam_init=lam_init)
            attn = jnp.concatenate([ap, a_s], axis=0)
            conv = [even_conv_w[j], even_conv_b[j].reshape(1, -1), even_conv_norm_g[j].reshape(1, -1),
                    even_conv_norm_b[j].reshape(1, -1)]
            xn, h2, route, cnt = _post(attn, u, conv, even_w_out[j].astype(BF16), xs, mod, g2, rwh, rwl,
                                       rbias, layer=layer, even=True, p_tok=p_tok, s_seq=dseq)
        else:
            q, k, v, kf, vf = _pre(xs, mod, g1, odd_w_qkv[j].astype(BF16), None, layer=layer,
                                   even=False, p_tok=p_tok, s_seq=dseq)
            outs["odd_k"] = kf.reshape(batch, 1, seq, C_HEADS, C_HEAD_DIM)
            outs["odd_v"] = vf.reshape(batch, 1, seq, C_HEADS, C_HEAD_DIM)
            n_pairs = d // LANE
            op = _pair_attn(q, k, v, batch=batch, n_pairs=n_pairs, nq_tok=seq, nk_tok=seq, q_row0=0,
                            k_row0=0, tq=seq, tk=seq)
            bias = _na_bias(odd_rpb[j], dseq // GRID_W)
            o_s = _na(q, k, v, cache_odd_k[:, j].reshape(dbatch, past, d).astype(BF16),
                      cache_odd_v[:, j].reshape(dbatch, past, d).astype(BF16), bias,
                      batch=dbatch, n_tok=dseq, row0=p_tok)
            o = jnp.concatenate([op, o_s], axis=0)
            xn, h2, route, cnt = _post(o, None, None, odd_w_out[j].astype(BF16), xs, mod, g2, rwh, rwl,
                                       rbias, layer=layer, even=False, p_tok=p_tok, s_seq=dseq)
        plan = _moe_plan(cnt, t_tok)
        x_sorted = _dispatch(plan, route, h2)
        y_sorted = _ffn(plan, x_sorted, moe_w1[layer].astype(BF16), moe_w3[layer].astype(BF16),
                        moe_w2[layer].astype(BF16))
        final = layer == depth - 1
        res = _combine(plan, route, y_sorted, xn, mod, final_g.reshape(1, d), layer=layer, final=final,
                       p_tok=p_tok, s_seq=dseq)
        if final:
            y_prompt, y_sample = res
        else:
            xs = [res]

    return (y_prompt.reshape(batch, seq, d), y_sample.reshape(dbatch, dseq, d),
            outs["even_k"], outs["even_v"], outs["odd_k"], outs["odd_v"])
```

```python
import functools

import numpy as np
import jax
import jax.numpy as jnp
from jax import lax
from jax.experimental import pallas as pl
from jax.experimental.pallas import tpu as pltpu

F32 = jnp.float32
BF16 = jnp.bfloat16
I32 = jnp.int32

GRID_W = 64
EPS = 1e-6
N_MOD = 6
A_HEADS = 4
A_QK_DIM = 64
A_V_DIM = 128
A_WIDTH = A_HEADS * A_V_DIM
CONV_WIDTH = 31
ROPE_BASE = 10000.0
ROPE_FREQS = A_QK_DIM // 4
C_HEADS = 16
C_HEAD_DIM = 64
NA_ROWS = 8
NA_COLS = 16
N_EXPERTS = 16
N_GROUPS = 4
EXPERTS_PER_GROUP = N_EXPERTS // N_GROUPS
NEG = -1e30

LANE = 128
BF16_ROWS = 16
VMEM_LIMIT = 52 * 1024 * 1024

TM = 256
TD = 512
TF = 256
HALO = 16
NA_QROWS = 4
NA_KROWS = NA_QROWS + NA_ROWS
NA_GROUPS = 2
PROMPT_GROUPS = 4
COND_ROWS = 16


def _nt(a, b):
    return lax.dot_general(a, b, (((1,), (1,)), ((), ())), preferred_element_type=F32)


def _mm(a, b):
    return jnp.dot(a, b, preferred_element_type=F32)


def _split_bf16(a):
    hi = a.astype(BF16)
    lo = (a - hi.astype(F32)).astype(BF16)
    return hi, lo


def _sigmoid(x):
    return 1.0 / (1.0 + jnp.exp(-x))


def _rms(x, g):
    return x * lax.rsqrt(jnp.mean(x * x, axis=-1, keepdims=True) + EPS) * g


def _ada_kernel(cond_ref, w_ref, b_ref, o_ref):
    c = cond_ref[...]
    a = c * _sigmoid(c)
    ah, al = _split_bf16(a)
    wh, wl = _split_bf16(w_ref[0])
    o_ref[0] = _mm(ah, wh) + _mm(ah, wl) + _mm(al, wh) + b_ref[0]


def _ada(cond, w_ada, b_ada):
    depth, d, n = w_ada.shape
    tn = 1536
    return pl.pallas_call(
        _ada_kernel,
        grid=(depth, n // tn),
        in_specs=[pl.BlockSpec((COND_ROWS, d), lambda l, j: (0, 0)),
                  pl.BlockSpec((1, d, tn), lambda l, j: (l, 0, j)),
                  pl.BlockSpec((1, 1, tn), lambda l, j: (l, 0, j))],
        out_specs=pl.BlockSpec((1, COND_ROWS, tn), lambda l, j: (l, 0, j)),
        out_shape=jax.ShapeDtypeStruct((depth, COND_ROWS, n), F32),
        compiler_params=pltpu.CompilerParams(dimension_semantics=("arbitrary", "arbitrary"),
                                             vmem_limit_bytes=VMEM_LIMIT),
        name="ada",
    )(cond, w_ada, b_ada.reshape(depth, 1, n))


def _load_x(x_refs, t, pt):
    if len(x_refs) == 1:
        return x_refs[0][...]
    return jnp.where(t < pt, x_refs[0][...], x_refs[1][...])


def _pre_even_kernel(*refs, n_x, pt):
    x_refs = refs[:n_x]
    (mod_ref, g_ref, w_ref, c_ref, s1_ref, s2_ref,
     q_ref, k_ref, v_ref, u_ref, kf_ref, vf_ref) = refs[n_x:]
    t = pl.program_id(0)
    x = _load_x(x_refs, t, pt)
    mod = mod_ref[0, 0]
    h = _rms(x, g_ref[...]) * (1.0 + mod[1:2]) + mod[0:1]
    proj = _mm(h.astype(BF16), w_ref[...])
    aw = A_WIDTH
    q = proj[:, :aw]
    k = proj[:, aw:2 * aw]
    v = proj[:, 2 * aw:3 * aw]
    bw = (proj.shape[1] - 3 * aw) // 2
    gv = proj[:, 3 * aw:3 * aw + bw]
    gg = proj[:, 3 * aw + bw:]
    v_ref[...] = v.astype(BF16)
    u_ref[...] = (gv * _sigmoid(gg)).astype(BF16)
    scale = A_QK_DIM ** -0.5

    @pl.when(t < pt)
    def _():
        q_ref[...] = (q * scale).astype(BF16)
        k_ref[...] = k.astype(BF16)
        kf_ref[...] = k
        vf_ref[...] = v

    @pl.when(t >= pt)
    def _():
        reps = aw // LANE
        c = jnp.concatenate([c_ref[...]] * reps, axis=1)
        s1 = jnp.concatenate([s1_ref[...]] * reps, axis=1)
        s2 = jnp.concatenate([s2_ref[...]] * reps, axis=1)

        def rope(z):
            return (z * c + pltpu.roll(z, ROPE_FREQS, 1) * s1
                    + pltpu.roll(z, aw - ROPE_FREQS, 1) * s2)

        q_ref[...] = (rope(q) * scale).astype(BF16)
        k_ref[...] = rope(k).astype(BF16)


def _pre_odd_kernel(*refs, n_x, pt):
    x_refs = refs[:n_x]
    mod_ref, g_ref, w_ref, q_ref, k_ref, v_ref, kf_ref, vf_ref = refs[n_x:]
    t = pl.program_id(0)
    x = _load_x(x_refs, t, pt)
    mod = mod_ref[0, 0]
    h = _rms(x, g_ref[...]) * (1.0 + mod[1:2]) + mod[0:1]
    proj = _mm(h.astype(BF16), w_ref[...])
    d = proj.shape[1] // 3
    q = proj[:, :d]
    k = proj[:, d:2 * d]
    v = proj[:, 2 * d:]
    q_ref[...] = (q * (C_HEAD_DIM ** -0.5)).astype(BF16)
    k_ref[...] = k.astype(BF16)
    v_ref[...] = v.astype(BF16)

    @pl.when(t < pt)
    def _():
        kf_ref[...] = k
        vf_ref[...] = v


def _x_specs(xs, pt, d):
    if len(xs) == 1:
        return [pl.BlockSpec((TM, d), lambda t: (t, 0))]
    return [pl.BlockSpec((TM, d), lambda t: (jnp.minimum(t, pt - 1), 0)),
            pl.BlockSpec((TM, d), lambda t: (jnp.maximum(t - pt, 0), 0))]


def _pre(xs, mod, g, w, rope, *, layer, even, p_tok, s_seq):
    d = xs[0].shape[1]
    t_tok = sum(x.shape[0] for x in xs) if len(xs) == 2 else xs[0].shape[0]
    nt = t_tok // TM
    pt = p_tok // TM
    st = s_seq // TM
    x_specs = _x_specs(xs, pt, d)
    mod_spec = pl.BlockSpec((1, 1, N_MOD, d),
                            lambda t: (layer, jnp.where(t < pt, 0, 1 + (t - pt) // st), 0, 0))
    g_spec = pl.BlockSpec((1, d), lambda t: (0, 0))
    w_spec = pl.BlockSpec(w.shape, lambda t: (0, 0))
    tok = lambda width: pl.BlockSpec((TM, width), lambda t: (t, 0))
    ptok = lambda width: pl.BlockSpec((TM, width), lambda t: (jnp.minimum(t, pt - 1), 0))
    params = pltpu.CompilerParams(dimension_semantics=("arbitrary",), vmem_limit_bytes=VMEM_LIMIT)
    if even:
        aw = A_WIDTH
        bw = (w.shape[1] - 3 * aw) // 2
        rspec = pl.BlockSpec((TM, LANE), lambda t: (jnp.maximum(t - pt, 0) % st, 0))
        return pl.pallas_call(
            functools.partial(_pre_even_kernel, n_x=len(xs), pt=pt),
            grid=(nt,),
            in_specs=x_specs + [mod_spec, g_spec, w_spec, rspec, rspec, rspec],
            out_specs=[tok(aw), tok(aw), tok(aw), tok(bw), ptok(aw), ptok(aw)],
            out_shape=[jax.ShapeDtypeStruct((t_tok, aw), BF16)] * 3
            + [jax.ShapeDtypeStruct((t_tok, bw), BF16)]
            + [jax.ShapeDtypeStruct((p_tok, aw), F32)] * 2,
            compiler_params=params, name="pre_even",
        )(*xs, mod, g, w, *rope)
    return pl.pallas_call(
        functools.partial(_pre_odd_kernel, n_x=len(xs), pt=pt),
        grid=(nt,),
        in_specs=x_specs + [mod_spec, g_spec, w_spec],
        out_specs=[tok(d), tok(d), tok(d), ptok(d), ptok(d)],
        out_shape=[jax.ShapeDtypeStruct((t_tok, d), BF16)] * 3
        + [jax.ShapeDtypeStruct((p_tok, d), F32)] * 2,
        compiler_params=params, name="pre_odd",
    )(*xs, mod, g, w)


def _rope_tables(n_tokens):
    t = jnp.arange(n_tokens, dtype=I32)
    pos = jnp.stack([t // GRID_W, t % GRID_W], axis=-1).astype(F32)
    freqs = ROPE_BASE ** (-jnp.arange(ROPE_FREQS, dtype=F32) / ROPE_FREQS)
    ang = pos[:, :, None] * freqs
    cos, sin = jnp.cos(ang), jnp.sin(ang)
    zero = jnp.zeros_like(sin)
    c = jnp.concatenate([cos, cos], axis=-1).reshape(n_tokens, A_QK_DIM)
    s1 = jnp.concatenate([zero, sin], axis=-1).reshape(n_tokens, A_QK_DIM)
    s2 = jnp.concatenate([-sin, zero], axis=-1).reshape(n_tokens, A_QK_DIM)
    rep = LANE // A_QK_DIM
    return tuple(jnp.tile(a, (1, rep)) for a in (c, s1, s2))


def _stack_maps(q):
    lane = lax.broadcasted_iota(I32, q.shape, 1)
    zero = jnp.zeros_like(q)
    half = LANE // 2
    return jnp.concatenate([jnp.where(lane < half, q, zero), jnp.where(lane >= half, q, zero)], axis=0)


def _pair_attn_kernel(*refs, diff, groups, tq, nk, tk, lam_init):
    if diff:
        q_ref, k_ref, v_ref, lam_ref, sg_ref, o_ref = refs
    else:
        q_ref, k_ref, v_ref, o_ref = refs
    half = LANE // 2
    for g in range(groups):
        qs = _stack_maps(q_ref[g * tq:(g + 1) * tq, :])
        k0 = g * nk * tk
        m = l = acc = None
        for j in range(nk):
            rows = slice(k0 + j * tk, k0 + (j + 1) * tk)
            s = _nt(qs, k_ref[rows, :])
            mc = jnp.max(s, axis=-1, keepdims=True)
            if j == 0:
                m = mc
                p = jnp.exp(s - m)
                l = jnp.sum(p, axis=-1, keepdims=True)
                acc = _mm(p.astype(BF16), v_ref[rows, :])
            else:
                m_new = jnp.maximum(m, mc)
                alpha = jnp.exp(m - m_new)
                p = jnp.exp(s - m_new)
                l = alpha * l + jnp.sum(p, axis=-1, keepdims=True)
                acc = alpha * acc + _mm(p.astype(BF16), v_ref[rows, :])
                m = m_new
        o = acc / l
        o1 = o[0:tq]
        o2 = o[tq:]
        if diff:
            lf = lam_ref[...]
            lam = (jnp.exp(jnp.sum(lf[0:1] * lf[1:2], axis=1, keepdims=True))
                   - jnp.exp(jnp.sum(lf[2:3] * lf[3:4], axis=1, keepdims=True)) + lam_init)
            dlt = o1 - lam * o2
            out = _rms(dlt, sg_ref[...]) * (1.0 - lam_init)
        else:
            lane = lax.broadcasted_iota(I32, o1.shape, 1)
            out = jnp.where(lane < half, o1, o2)
        o_ref[g * tq:(g + 1) * tq, :] = out.astype(o_ref.dtype)


def _pair_attn(q, k, v, *, batch, n_pairs, nq_tok, nk_tok, q_row0, k_row0, tq, tk, groups,
               diff=False, lam=None, subln=None, lam_init=0.0):
    nk = nk_tok // tk
    if groups > 1:
        assert nq_tok == tq and batch % groups == 0
        nb, nq = batch // groups, 1
    else:
        nb, nq = batch, nq_tok // tq
    qblk, kblk = groups * tq, groups * nk_tok
    qb0, kb0 = q_row0 // qblk, k_row0 // kblk
    in_specs = [pl.BlockSpec((qblk, LANE), lambda b, h, qi: (qb0 + b * nq + qi, h)),
                pl.BlockSpec((kblk, LANE), lambda b, h, qi: (kb0 + b, h)),
                pl.BlockSpec((kblk, LANE), lambda b, h, qi: (kb0 + b, h))]
    args = [q, k, v]
    if diff:
        in_specs += [pl.BlockSpec(lam.shape, lambda b, h, qi: (0, 0)),
                     pl.BlockSpec(subln.shape, lambda b, h, qi: (0, 0))]
        args += [lam, subln]
    return pl.pallas_call(
        functools.partial(_pair_attn_kernel, diff=diff, groups=groups, tq=tq, nk=nk, tk=tk,
                          lam_init=lam_init),
        grid=(nb, n_pairs, nq),
        in_specs=in_specs,
        out_specs=pl.BlockSpec((qblk, LANE), lambda b, h, qi: (b * nq + qi, h)),
        out_shape=jax.ShapeDtypeStruct((batch * nq_tok, n_pairs * LANE), BF16),
        compiler_params=pltpu.CompilerParams(
            dimension_semantics=("arbitrary",) * 3, vmem_limit_bytes=VMEM_LIMIT),
        name="diff_attn" if diff else "pair_attn",
    )(*args)


def _na_kernel(q_ref, k_ref, v_ref, kc_ref, vc_ref, bias_ref, o_ref, *, nrb, n_tok):
    tq = NA_QROWS * GRID_W
    tkw = NA_KROWS * GRID_W
    half = LANE // 2
    for g in range(NA_GROUPS):
        rb = pl.program_id(2) * NA_GROUPS + g
        qs = _stack_maps(q_ref[g * tq:(g + 1) * tq, :])
        start = pl.multiple_of(jnp.clip(rb * tq - tq, 0, n_tok - tkw), tq)
        case = jnp.where(rb == 0, 0, jnp.where(rb == nrb - 1, 2, 1))
        bias = jnp.concatenate([bias_ref[0, case], bias_ref[1, case]], axis=0)
        s_loc = _nt(qs, k_ref[pl.ds(start, tkw), :]) + bias
        s_ctx = _nt(qs, kc_ref[0])
        m = jnp.maximum(jnp.max(s_loc, axis=-1, keepdims=True), jnp.max(s_ctx, axis=-1, keepdims=True))
        p_loc = jnp.exp(s_loc - m)
        p_ctx = jnp.exp(s_ctx - m)
        l = jnp.sum(p_loc, axis=-1, keepdims=True) + jnp.sum(p_ctx, axis=-1, keepdims=True)
        o = (_mm(p_loc.astype(BF16), v_ref[pl.ds(start, tkw), :])
             + _mm(p_ctx.astype(BF16), vc_ref[0])) / l
        lane = lax.broadcasted_iota(I32, (tq, LANE), 1)
        o_ref[g * tq:(g + 1) * tq, :] = jnp.where(lane < half, o[0:tq], o[tq:]).astype(o_ref.dtype)


def _na_bias(rpb, rows):
    nrb = rows // NA_QROWS
    h = rpb.shape[0]
    nd = 2 * NA_COLS - 1
    c = np.arange(GRID_W)[:, None]
    kc = np.arange(GRID_W)[None, :]
    cs = np.clip(c - NA_COLS // 2, 0, GRID_W - NA_COLS)
    okc = (kc >= cs) & (kc < cs + NA_COLS)
    dc = np.where(okc, kc - c + NA_COLS - 1, -1)
    onehot = (dc.reshape(1, -1) == np.arange(nd)[:, None]).astype(np.float32)
    colmask = np.where(okc, 0.0, NEG).astype(np.float32).reshape(1, 1, GRID_W, GRID_W)
    colb = jnp.dot(rpb.astype(F32).reshape(-1, nd), onehot, precision=lax.Precision.HIGHEST)
    colb = colb.reshape(h, 2 * NA_ROWS - 1, GRID_W, GRID_W) + colmask
    masked = jnp.full((h, GRID_W, GRID_W), NEG, F32)
    cases = []
    for rb in (0, 1, nrb - 1):
        start = int(np.clip(NA_QROWS * rb - NA_QROWS, 0, rows - NA_KROWS))
        qrows = []
        for i in range(NA_QROWS):
            r = NA_QROWS * rb + i
            rs = int(np.clip(r - NA_ROWS // 2, 0, rows - NA_ROWS))
            blocks = []
            for j in range(NA_KROWS):
                kr = start + j
                blocks.append(colb[:, kr - r + NA_ROWS - 1] if rs <= kr < rs + NA_ROWS else masked)
            qrows.append(jnp.stack(blocks, axis=2).reshape(h, GRID_W, NA_KROWS * GRID_W))
        cases.append(jnp.concatenate(qrows, axis=1))
    return jnp.stack(cases, axis=1)


def _na(q, k, v, kc, vc, bias, *, batch, n_tok, row0):
    d = q.shape[1]
    n_pairs = d // LANE
    tq = NA_QROWS * GRID_W
    nrb = n_tok // tq
    past = kc.shape[1]
    qblk = NA_GROUPS * tq
    ns = nrb // NA_GROUPS
    qb0 = row0 // qblk
    kb0 = row0 // n_tok
    return pl.pallas_call(
        functools.partial(_na_kernel, nrb=nrb, n_tok=n_tok),
        grid=(n_pairs, batch, ns),
        in_specs=[pl.BlockSpec((qblk, LANE), lambda h, b, r: (qb0 + b * ns + r, h)),
                  pl.BlockSpec((n_tok, LANE), lambda h, b, r: (kb0 + b, h)),
                  pl.BlockSpec((n_tok, LANE), lambda h, b, r: (kb0 + b, h)),
                  pl.BlockSpec((1, past, LANE), lambda h, b, r: (b, 0, h)),
                  pl.BlockSpec((1, past, LANE), lambda h, b, r: (b, 0, h)),
                  pl.BlockSpec((2,) + bias.shape[1:], lambda h, b, r: (h, 0, 0, 0))],
        out_specs=pl.BlockSpec((qblk, LANE), lambda h, b, r: (b * ns + r, h)),
        out_shape=jax.ShapeDtypeStruct((batch * n_tok, d), BF16),
        compiler_params=pltpu.CompilerParams(
            dimension_semantics=("arbitrary",) * 3, vmem_limit_bytes=VMEM_LIMIT),
        name="na_attn",
    )(q, k, v, kc, vc, bias)


def _route(h2, rwh_ref, rwl_ref, rb_ref, route_ref, cnt_ref):
    hh, hl = _split_bf16(h2)
    logits = _nt(rwh_ref[...], hh) + _nt(rwh_ref[...], hl) + _nt(rwl_ref[...], hh)
    scores = _sigmoid(logits)
    sel = scores + rb_ref[...]
    tm = sel.shape[1]
    gs = []
    for g in range(N_GROUPS):
        r = [sel[g * EXPERTS_PER_GROUP + a:g * EXPERTS_PER_GROUP + a + 1] for a in range(EXPERTS_PER_GROUP)]
        best = None
        for a in range(EXPERTS_PER_GROUP):
            for b in range(a + 1, EXPERTS_PER_GROUP):
                pair = r[a] + r[b]
                best = pair if best is None else jnp.maximum(best, pair)
        gs.append(best)
    bg = jnp.zeros((1, tm), I32)
    bs = gs[0]
    for g in range(1, N_GROUPS):
        better = gs[g] > bs
        bg = jnp.where(better, g, bg)
        bs = jnp.where(better, gs[g], bs)
    eidx = lax.broadcasted_iota(I32, sel.shape, 0)
    masked = jnp.where(eidx // EXPERTS_PER_GROUP == bg, sel, -jnp.inf)
    m1 = jnp.max(masked, axis=0, keepdims=True)
    i1 = jnp.min(jnp.where(masked == m1, eidx, N_EXPERTS), axis=0, keepdims=True)
    masked2 = jnp.where(eidx == i1, -jnp.inf, masked)
    m2 = jnp.max(masked2, axis=0, keepdims=True)
    i2 = jnp.min(jnp.where(masked2 == m2, eidx, N_EXPERTS), axis=0, keepdims=True)
    s1 = jnp.sum(jnp.where(eidx == i1, scores, 0.0), axis=0, keepdims=True)
    s2 = jnp.sum(jnp.where(eidx == i2, scores, 0.0), axis=0, keepdims=True)
    tot = s1 + s2
    route_ref[0] = jnp.concatenate(
        [i1.astype(F32), i2.astype(F32), s1 / tot, s2 / tot, jnp.zeros((4, tm), F32)], axis=0)
    chosen = jnp.where(eidx == i1, 1.0, jnp.where(eidx == i2, 1.0, 0.0))
    cnt_ref[0] = jnp.broadcast_to(jnp.sum(chosen, axis=1, keepdims=True), cnt_ref.shape[1:])


def _post_kernel(*refs, even, n_x, pt, st):
    if even:
        (a_ref, up_ref, uc_ref, un_ref, cw_ref, cb_ref, cg_ref, cnb_ref, wo_ref) = refs[:9]
        rest = refs[9:]
    else:
        a_ref, wo_ref = refs[:2]
        rest = refs[2:]
    x_refs = rest[:n_x]
    (mod_ref, g2_ref, rwh_ref, rwl_ref, rb_ref, xn_ref, h2_ref, route_ref, cnt_ref) = rest[n_x:n_x + 9]
    t = pl.program_id(0)
    if even:
        ext = rest[n_x + 9]
        js = jnp.maximum(t - pt, 0) % st
        has_prev = jnp.logical_and(t >= pt, js != 0)
        has_next = jnp.logical_and(t >= pt, js != st - 1)
        up = up_ref[...].astype(F32)
        un = un_ref[...].astype(F32)
        ext[0:HALO, :] = jnp.where(has_prev, up, jnp.zeros_like(up))
        ext[HALO:HALO + TM, :] = uc_ref[...].astype(F32)
        ext[HALO + TM:, :] = jnp.where(has_next, un, jnp.zeros_like(un))
        pad = CONV_WIDTH // 2
        acc = jnp.zeros((TM, ext.shape[1]), F32) + cb_ref[...]
        for j in range(CONV_WIDTH):
            acc = acc + cw_ref[j:j + 1, :] * ext[pl.ds(HALO - pad + j, TM), :]
        mu = jnp.mean(acc, axis=-1, keepdims=True)
        var = jnp.mean(jnp.square(acc - mu), axis=-1, keepdims=True)
        y = (acc - mu) * lax.rsqrt(var + EPS) * cg_ref[...] + cnb_ref[...]
        ua = (y * _sigmoid(y)).astype(BF16)
        aw = a_ref.shape[1]
        mix = _mm(a_ref[...], wo_ref[0:aw, :]) + _mm(ua, wo_ref[aw:, :])
    else:
        mix = _mm(a_ref[...], wo_ref[...])
    x = _load_x(x_refs, t, pt)
    mod = mod_ref[0, 0]
    xn = x + mod[2:3] * mix
    xn_ref[...] = xn
    h2 = _rms(xn, g2_ref[...]) * (1.0 + mod[4:5]) + mod[3:4]
    h2_ref[...] = h2.astype(BF16)
    _route(h2, rwh_ref, rwl_ref, rb_ref, route_ref, cnt_ref)


def _post(a, u, conv, wo, xs, mod, g2, rwh, rwl, rbias, *, layer, even, p_tok, s_seq):
    d = wo.shape[1]
    t_tok = a.shape[0]
    nt = t_tok // TM
    pt = p_tok // TM
    st = s_seq // TM
    tok = lambda width: pl.BlockSpec((TM, width), lambda t: (t, 0))
    full = lambda arr: pl.BlockSpec(arr.shape, lambda t: (0,) * arr.ndim)
    in_specs, args, scratch = [], [], []
    if even:
        bw = u.shape[1]
        hb = TM // HALO
        in_specs += [tok(a.shape[1]),
                     pl.BlockSpec((HALO, bw), lambda t: (jnp.maximum(t * hb - 1, 0), 0)),
                     tok(bw),
                     pl.BlockSpec((HALO, bw), lambda t: (jnp.minimum((t + 1) * hb, nt * hb - 1), 0))]
        args += [a, u, u, u]
        for arr in conv:
            in_specs.append(full(arr))
            args.append(arr)
        in_specs.append(full(wo))
        args.append(wo)
        scratch = [pltpu.VMEM((TM + 2 * HALO, bw), F32)]
    else:
        in_specs += [tok(a.shape[1]), full(wo)]
        args += [a, wo]
    in_specs += _x_specs(xs, pt, d)
    args += list(xs)
    in_specs += [pl.BlockSpec((1, 1, N_MOD, d),
                              lambda t: (layer, jnp.where(t < pt, 0, 1 + (t - pt) // st), 0, 0)),
                 full(g2), full(rwh), full(rwl), full(rbias)]
    args += [mod, g2, rwh, rwl, rbias]
    return pl.pallas_call(
        functools.partial(_post_kernel, even=even, n_x=len(xs), pt=pt, st=st),
        grid=(nt,),
        in_specs=in_specs,
        out_specs=[tok(d), tok(d),
                   pl.BlockSpec((1, 8, TM), lambda t: (t, 0, 0)),
                   pl.BlockSpec((1, N_EXPERTS, LANE), lambda t: (t, 0, 0))],
        out_shape=[jax.ShapeDtypeStruct((t_tok, d), F32), jax.ShapeDtypeStruct((t_tok, d), BF16),
                   jax.ShapeDtypeStruct((nt, 8, TM), F32),
                   jax.ShapeDtypeStruct((nt, N_EXPERTS, LANE), F32)],
        scratch_shapes=scratch,
        compiler_params=pltpu.CompilerParams(dimension_semantics=("arbitrary",),
                                             vmem_limit_bytes=VMEM_LIMIT),
        name="post_even" if even else "post_odd",
    )(*args)


def _stage_rows():
    return 2 * TD + N_EXPERTS * BF16_ROWS


def _chunk_bits():
    bits = []
    b = TD
    while b >= BF16_ROWS:
        bits.append(b)
        b //= 2
    return bits


def _moe_plan(cnt, t_tok):
    nd = t_tok // TD
    cnt = cnt[:, :, 0].astype(I32).reshape(nd, TD // TM, N_EXPERTS).sum(axis=1)
    cnt16 = (cnt + BF16_ROWS - 1) // BF16_ROWS * BF16_ROWS
    loff = jnp.cumsum(cnt16, axis=1) - cnt16
    tot = cnt16.sum(axis=0)
    reg = (tot + TF - 1) // TF * TF
    off = jnp.cumsum(reg) - reg
    goff = off[None, :] + jnp.cumsum(cnt16, axis=0) - cnt16
    ends = jnp.cumsum(reg // TF)
    n_used = ends[-1]
    n_tiles = _sorted_rows(t_tok) // TF
    tile = jnp.minimum(jnp.arange(n_tiles, dtype=I32), n_used - 1)
    tile_expert = jnp.minimum(jnp.sum(tile[:, None] >= ends[None, :], axis=1), N_EXPERTS - 1).astype(I32)
    tail = jnp.stack([n_used * TF, n_tiles - n_used])
    return dict(cnt16=cnt16.reshape(-1), loff=loff.reshape(-1), goff=goff.reshape(-1),
                gap_start=jnp.concatenate([off + tot, tail[0:1]]).astype(I32),
                gap_len=jnp.concatenate([reg - tot, tail[1:2]]).astype(I32),
                tile=tile.astype(I32), tile_expert=tile_expert,
                n_used=jnp.reshape(n_used, (1,)).astype(I32))


def _sorted_rows(t_tok):
    nd = t_tok // TD
    worst = 2 * t_tok + nd * N_EXPERTS * (BF16_ROWS - 1) + N_EXPERTS * (TF - 1)
    return (worst + TF - 1) // TF * TF


def _route_rows(route_ref):
    parts = [route_ref[j] for j in range(route_ref.shape[0])]
    return parts[0] if len(parts) == 1 else jnp.concatenate(parts, axis=1)


def _perm(route, loff_ref, tile):
    e0 = route[0:1].astype(I32)
    e1 = route[1:2].astype(I32)
    eidx = lax.broadcasted_iota(I32, (N_EXPERTS, TD), 0)
    m0 = eidx == e0
    m1 = eidx == e1
    chosen = jnp.where(m0, 1.0, jnp.where(m1, 1.0, 0.0)).astype(BF16)
    before = (lax.broadcasted_iota(I32, (TD, TD), 0) < lax.broadcasted_iota(I32, (TD, TD), 1))
    rank = _mm(chosen, jnp.where(before, 1.0, 0.0).astype(BF16))
    ecol = lax.broadcasted_iota(I32, (N_EXPERTS, 1), 0)
    lcol = jnp.zeros((N_EXPERTS, 1), F32)
    for e in range(N_EXPERTS):
        lcol = jnp.where(ecol == e, loff_ref[tile * N_EXPERTS + e].astype(F32), lcol)
    base = rank + lcol
    slot0 = jnp.sum(jnp.where(m0, base, 0.0), axis=0, keepdims=True).astype(I32)
    slot1 = jnp.sum(jnp.where(m1, base, 0.0), axis=0, keepdims=True).astype(I32)
    rio = lax.broadcasted_iota(I32, (_stage_rows(), TD), 0)
    return rio == slot0, rio == slot1


def _chunk_copies(cnt_ref, loff_ref, goff_ref, tile, stage, sorted_hbm, sem, *, to_sorted, wait):
    for e in range(N_EXPERTS):
        n = cnt_ref[tile * N_EXPERTS + e]
        lo = loff_ref[tile * N_EXPERTS + e]
        go = goff_ref[tile * N_EXPERTS + e]
        for bit in _chunk_bits():
            done = n & ~(2 * bit - 1)
            s_view = stage.at[pl.ds(pl.multiple_of(lo + done, BF16_ROWS), bit), :]
            h_view = sorted_hbm.at[pl.ds(pl.multiple_of(go + done, BF16_ROWS), bit), :]
            cp = (pltpu.make_async_copy(s_view, h_view, sem) if to_sorted
                  else pltpu.make_async_copy(h_view, s_view, sem))

            @pl.when((n & bit) != 0)
            def _(cp=cp):
                cp.wait() if wait else cp.start()


def _dispatch_kernel(cnt_ref, loff_ref, goff_ref, gs_ref, gl_ref,
                     route_ref, h2_ref, xs_hbm, stage, zeros, sem, zsem):
    i = pl.program_id(0)
    n = pl.num_programs(0)
    slot = i % 2
    p0, p1 = _perm(_route_rows(route_ref), loff_ref, i)
    p = jnp.where(p0, 1.0, jnp.where(p1, 1.0, 0.0)).astype(BF16)
    stage[slot] = _mm(p, h2_ref[...]).astype(BF16)
    copies = functools.partial(_chunk_copies, cnt_ref, loff_ref, goff_ref, to_sorted=True)
    copies(i, stage.at[slot], xs_hbm, sem.at[slot], wait=False)

    @pl.when(i > 0)
    def _():
        copies(i - 1, stage.at[1 - slot], xs_hbm, sem.at[1 - slot], wait=True)

    @pl.when(i == n - 1)
    def _():
        zeros[...] = jnp.zeros(zeros.shape, zeros.dtype)
        bits = [b for b in _chunk_bits() if b < TF]
        for wait in (False, True):
            for e in range(N_EXPERTS):
                g0, gl = gs_ref[e], gl_ref[e]
                for bit in bits:
                    done = gl & ~(2 * bit - 1)
                    cp = pltpu.make_async_copy(
                        zeros.at[pl.ds(0, bit), :],
                        xs_hbm.at[pl.ds(pl.multiple_of(g0 + done, BF16_ROWS), bit), :], zsem)

                    @pl.when((gl & bit) != 0)
                    def _(cp=cp, wait=wait):
                        cp.wait() if wait else cp.start()
        tail0, ntail = gs_ref[N_EXPERTS], gl_ref[N_EXPERTS]

        def tail_copy(j):
            return pltpu.make_async_copy(
                zeros, xs_hbm.at[pl.ds(pl.multiple_of(tail0 + j * TF, TF), TF), :], zsem)

        @pl.loop(0, ntail)
        def _(j):
            tail_copy(j).start()

        @pl.loop(0, ntail)
        def _(j):
            tail_copy(j).wait()

        copies(i, stage.at[slot], xs_hbm, sem.at[slot], wait=True)


def _dispatch(plan, route, h2):
    t_tok, d = h2.shape
    nd = t_tok // TD
    k = TD // TM
    rows = _sorted_rows(t_tok)
    grid_spec = pltpu.PrefetchScalarGridSpec(
        num_scalar_prefetch=5,
        grid=(nd,),
        in_specs=[pl.BlockSpec((k, 8, TM), lambda i, *_: (i, 0, 0)),
                  pl.BlockSpec((TD, d), lambda i, *_: (i, 0))],
        out_specs=pl.BlockSpec(memory_space=pl.ANY),
        scratch_shapes=[pltpu.VMEM((2, _stage_rows(), d), BF16), pltpu.VMEM((TF, d), BF16),
                        pltpu.SemaphoreType.DMA((2,)), pltpu.SemaphoreType.DMA(())],
    )
    return pl.pallas_call(
        _dispatch_kernel,
        grid_spec=grid_spec,
        out_shape=jax.ShapeDtypeStruct((rows, d), BF16),
        compiler_params=pltpu.CompilerParams(dimension_semantics=("arbitrary",),
                                             vmem_limit_bytes=VMEM_LIMIT),
        name="moe_dispatch",
    )(plan["cnt16"], plan["loff"], plan["goff"], plan["gap_start"], plan["gap_len"], route, h2)


def _ffn_kernel(tile_ref, te_ref, nu_ref, x_ref, w1_ref, w3_ref, w2_ref, y_ref):
    i = pl.program_id(0)

    @pl.when(i < nu_ref[0])
    def _():
        x = x_ref[...]
        a = _mm(x, w1_ref[0])
        b = _mm(x, w3_ref[0])
        he = (a * _sigmoid(a) * b).astype(BF16)
        y_ref[...] = _mm(he, w2_ref[0]).astype(y_ref.dtype)

    @pl.when(i >= nu_ref[0])
    def _():
        y_ref[...] = jnp.zeros(y_ref.shape, y_ref.dtype)


def _ffn(plan, xs, w1, w3, w2):
    rows, d = xs.shape
    f = w1.shape[2]
    grid_spec = pltpu.PrefetchScalarGridSpec(
        num_scalar_prefetch=3,
        grid=(rows // TF,),
        in_specs=[pl.BlockSpec((TF, d), lambda i, tile, te, nu: (tile[i], 0)),
                  pl.BlockSpec((1, d, f), lambda i, tile, te, nu: (te[i], 0, 0)),
                  pl.BlockSpec((1, d, f), lambda i, tile, te, nu: (te[i], 0, 0)),
                  pl.BlockSpec((1, f, d), lambda i, tile, te, nu: (te[i], 0, 0))],
        out_specs=pl.BlockSpec((TF, d), lambda i, tile, te, nu: (i, 0)),
    )
    return pl.pallas_call(
        _ffn_kernel,
        grid_spec=grid_spec,
        out_shape=jax.ShapeDtypeStruct((rows, d), BF16),
        compiler_params=pltpu.CompilerParams(dimension_semantics=("arbitrary",),
                                             vmem_limit_bytes=VMEM_LIMIT),
        name="moe_ffn",
    )(plan["tile"], plan["tile_expert"], plan["n_used"], xs, w1, w3, w2)


def _combine_kernel(*refs, final, ptd):
    cnt_ref, loff_ref, goff_ref, route_ref, ys_hbm, xn_ref, mod_ref = refs[:7]
    if final:
        fg_ref, yp_ref, ysm_ref, stage, sem = refs[7:]
    else:
        xo_ref, stage, sem = refs[7:]
    i = pl.program_id(0)
    n = pl.num_programs(0)
    slot = i % 2
    copies = functools.partial(_chunk_copies, cnt_ref, loff_ref, goff_ref, to_sorted=False)

    @pl.when(i == 0)
    def _():
        stage[...] = jnp.zeros(stage.shape, stage.dtype)
        copies(i, stage.at[0], ys_hbm, sem.at[0], wait=False)

    @pl.when(i + 1 < n)
    def _():
        copies(i + 1, stage.at[1 - slot], ys_hbm, sem.at[1 - slot], wait=False)

    route = _route_rows(route_ref)
    p0, p1 = _perm(route, loff_ref, i)
    gate_rows = jnp.where(p0, route[2:3], jnp.where(p1, route[3:4], 0.0))
    gate_col = jnp.sum(gate_rows, axis=1, keepdims=True)
    p = jnp.where(p0, 1.0, jnp.where(p1, 1.0, 0.0)).astype(BF16)
    copies(i, stage.at[slot], ys_hbm, sem.at[slot], wait=True)
    scaled = (stage[slot].astype(F32) * gate_col).astype(BF16)
    y = lax.dot_general(p, scaled, (((0,), (0,)), ((), ())), preferred_element_type=F32)
    xo = xn_ref[...] + mod_ref[0, 0][5:6] * y
    if final:
        out = _rms(xo, fg_ref[...])

        @pl.when(i < ptd)
        def _():
            yp_ref[...] = out

        @pl.when(i >= ptd)
        def _():
            ysm_ref[...] = out
    else:
        xo_ref[...] = xo


def _combine(plan, route, ys, xn, mod, final_g, *, layer, final, p_tok, s_seq):
    t_tok, d = xn.shape
    nd = t_tok // TD
    k = TD // TM
    ptd = p_tok // TD
    std = s_seq // TD
    in_specs = [pl.BlockSpec((k, 8, TM), lambda i, *_: (i, 0, 0)),
                pl.BlockSpec(memory_space=pl.ANY),
                pl.BlockSpec((TD, d), lambda i, *_: (i, 0)),
                pl.BlockSpec((1, 1, N_MOD, d),
                             lambda i, *_: (layer, jnp.where(i < ptd, 0, 1 + (i - ptd) // std), 0, 0))]
    args = [route, ys, xn, mod]
    if final:
        in_specs.append(pl.BlockSpec((1, d), lambda i, *_: (0, 0)))
        args.append(final_g)
        out_specs = [pl.BlockSpec((TD, d), lambda i, *_: (jnp.minimum(i, ptd - 1), 0)),
                     pl.BlockSpec((TD, d), lambda i, *_: (jnp.maximum(i - ptd, 0), 0))]
        out_shape = [jax.ShapeDtypeStruct((p_tok, d), F32),
                     jax.ShapeDtypeStruct((t_tok - p_tok, d), F32)]
    else:
        out_specs = pl.BlockSpec((TD, d), lambda i, *_: (i, 0))
        out_shape = jax.ShapeDtypeStruct((t_tok, d), F32)
    grid_spec = pltpu.PrefetchScalarGridSpec(
        num_scalar_prefetch=3, grid=(nd,), in_specs=in_specs, out_specs=out_specs,
        scratch_shapes=[pltpu.VMEM((2, _stage_rows(), d), BF16), pltpu.SemaphoreType.DMA((2,))])
    return pl.pallas_call(
        functools.partial(_combine_kernel, final=final, ptd=ptd),
        grid_spec=grid_spec, out_shape=out_shape,
        compiler_params=pltpu.CompilerParams(dimension_semantics=("arbitrary",),
                                             vmem_limit_bytes=VMEM_LIMIT),
        name="moe_combine_final" if final else "moe_combine",
    )(plan["cnt16"], plan["loff"], plan["goff"], *args)


def kernel(x_prompt, x_sample, cache_even_k, cache_even_v, cache_odd_k, cache_odd_v, c, c_ctx, w_ada, b_ada, norm_g, final_g, even_w_in, even_w_out, even_lam, even_subln_g, even_conv_w, even_conv_b, even_conv_norm_g, even_conv_norm_b, odd_w_qkv, odd_w_out, odd_rpb, router_w, router_b, moe_w1, moe_w3, moe_w2):
    batch, seq, d = x_prompt.shape
    dbatch, dseq, _ = x_sample.shape
    past = cache_even_k.shape[2]
    depth = w_ada.shape[0]
    p_tok, s_tok = batch * seq, dbatch * dseq
    t_tok = p_tok + s_tok
    assert seq == TM and dseq % TD == 0 and p_tok % TD == 0 and p_tok % dseq == 0
    assert 1 + dbatch <= COND_ROWS and dseq % GRID_W == 0 and past % TM == 0
    assert dseq // GRID_W >= NA_KROWS + NA_QROWS

    cond = jnp.zeros((COND_ROWS, d), F32).at[0].set(c_ctx).at[1:1 + dbatch].set(c)
    mod = _ada(cond, w_ada, b_ada).reshape(depth, COND_ROWS, N_MOD, d)
    rope = _rope_tables(dseq)
    rwh, rwl = _split_bf16(router_w.T)
    rbias = router_b.reshape(N_EXPERTS, 1)
    xs = [x_prompt.reshape(p_tok, d), x_sample.reshape(s_tok, d)]
    outs = {}

    for layer in range(depth):
        j = layer // 2
        even = layer % 2 == 0
        g1 = norm_g[layer, 0].reshape(1, d)
        g2 = norm_g[layer, 1].reshape(1, d)
        if even:
            q, k, v, u, kf, vf = _pre(xs, mod, g1, even_w_in[j].astype(BF16), rope, layer=layer,
                                      even=True, p_tok=p_tok, s_seq=dseq)
            outs["even_k"] = kf.reshape(batch, 1, seq, A_HEADS, A_V_DIM)
            outs["even_v"] = vf.reshape(batch, 1, seq, A_HEADS, A_V_DIM)
            lam_init = 0.8 - 0.6 * float(np.exp(-0.3 * layer))
            lam = even_lam[j]
            subln = even_subln_g[j].reshape(1, A_V_DIM)
            ap = _pair_attn(q, k, v, batch=batch, n_pairs=A_HEADS, nq_tok=seq, nk_tok=seq, q_row0=0,
                            k_row0=0, tq=seq, tk=seq, groups=PROMPT_GROUPS, diff=True, lam=lam,
                            subln=subln, lam_init=lam_init)
            width = A_WIDTH
            k_all = jnp.concatenate([cache_even_k[:, j].reshape(dbatch, past, width).astype(BF16),
                                     k[p_tok:].reshape(dbatch, dseq, width)], axis=1)
            v_all = jnp.concatenate([cache_even_v[:, j].reshape(dbatch, past, width).astype(BF16),
                                     v[p_tok:].reshape(dbatch, dseq, width)], axis=1)
            nk_tok = past + dseq
            a_s = _pair_attn(q, k_all.reshape(dbatch * nk_tok, width), v_all.reshape(dbatch * nk_tok, width),
                             batch=dbatch, n_pairs=A_HEADS, nq_tok=dseq, nk_tok=nk_tok, q_row0=p_tok,
                             k_row0=0, tq=TM, tk=TD, groups=1, diff=True, lam=lam, subln=subln,
                             lam_init=lam_init)
            attn = jnp.concatenate([ap, a_s], axis=0)
            conv = [even_conv_w[j], even_conv_b[j].reshape(1, -1), even_conv_norm_g[j].reshape(1, -1),
                    even_conv_norm_b[j].reshape(1, -1)]
            xn, h2, route, cnt = _post(attn, u, conv, even_w_out[j].astype(BF16), xs, mod, g2, rwh, rwl,
                                       rbias, layer=layer, even=True, p_tok=p_tok, s_seq=dseq)
        else:
            q, k, v, kf, vf = _pre(xs, mod, g1, odd_w_qkv[j].astype(BF16), None, layer=layer,
                                   even=False, p_tok=p_tok, s_seq=dseq)
            outs["odd_k"] = kf.reshape(batch, 1, seq, C_HEADS, C_HEAD_DIM)
            outs["odd_v"] = vf.reshape(batch, 1, seq, C_HEADS, C_HEAD_DIM)
            n_pairs = d // LANE
            op = _pair_attn(q, k, v, batch=batch, n_pairs=n_pairs, nq_tok=seq, nk_tok=seq, q_row0=0,
                            k_row0=0, tq=seq, tk=seq, groups=PROMPT_GROUPS)
            bias = _na_bias(odd_rpb[j], dseq // GRID_W)
            o_s = _na(q, k, v, cache_odd_k[:, j].reshape(dbatch, past, d).astype(BF16),
                      cache_odd_v[:, j].reshape(dbatch, past, d).astype(BF16), bias,
                      batch=dbatch, n_tok=dseq, row0=p_tok)
            o = jnp.concatenate([op, o_s], axis=0)
            xn, h2, route, cnt = _post(o, None, None, odd_w_out[j].astype(BF16), xs, mod, g2, rwh, rwl,
                                       rbias, layer=layer, even=False, p_tok=p_tok, s_seq=dseq)
        plan = _moe_plan(cnt, t_tok)
        x_sorted = _dispatch(plan, route, h2)
        y_sorted = _ffn(plan, x_sorted, moe_w1[layer].astype(BF16), moe_w3[layer].astype(BF16),
                        moe_w2[layer].astype(BF16))
        final = layer == depth - 1
        res = _combine(plan, route, y_sorted, xn, mod, final_g.reshape(1, d), layer=layer, final=final,
                       p_tok=p_tok, s_seq=dseq)
        if final:
            y_prompt, y_sample = res
        else:
            xs = [res]

    return (y_prompt.reshape(batch, seq, d), y_sample.reshape(dbatch, dseq, d),
            outs["even_k"], outs["even_v"], outs["odd_k"], outs["odd_v"])
```

```python
import functools

import numpy as np
import jax
import jax.numpy as jnp
from jax import lax
from jax.experimental import pallas as pl
from jax.experimental.pallas import tpu as pltpu

F32 = jnp.float32
BF16 = jnp.bfloat16
I32 = jnp.int32

GRID_W = 64
EPS = 1e-6
N_MOD = 6
A_HEADS = 4
A_QK_DIM = 64
A_V_DIM = 128
A_WIDTH = A_HEADS * A_V_DIM
CONV_WIDTH = 31
ROPE_BASE = 10000.0
ROPE_FREQS = A_QK_DIM // 4
C_HEADS = 16
C_HEAD_DIM = 64
NA_ROWS = 8
NA_COLS = 16
N_EXPERTS = 16
N_GROUPS = 4
EXPERTS_PER_GROUP = N_EXPERTS // N_GROUPS
NEG = -1e30
LOG2E = 1.4426950408889634

LANE = 128
SUBLANE = 8
BF16_ROWS = 16
VMEM_LIMIT = 52 * 1024 * 1024

TM = 512
ATT_TQ = 256
ATT_TK = 512
TD = 512
TF = 512
HALO = 16
CONV_ROWS = 64
ROUTE_TILES = 8
NA_QROWS = 4
NA_KROWS = NA_QROWS + NA_ROWS
NA_GROUPS = 2
PROMPT_GROUPS = 4
COND_ROWS = 16


def _nt(a, b):
    return lax.dot_general(a, b, (((1,), (1,)), ((), ())), preferred_element_type=F32)


def _mm(a, b):
    return jnp.dot(a, b, preferred_element_type=F32)


def _split_bf16(a):
    hi = a.astype(BF16)
    lo = (a - hi.astype(F32)).astype(BF16)
    return hi, lo


def _sigmoid(x):
    return 1.0 / (1.0 + jnp.exp(-x))


def _rms(x, g):
    return x * lax.rsqrt(jnp.mean(x * x, axis=-1, keepdims=True) + EPS) * g


def _ada_kernel(cond_ref, w_ref, b_ref, o_ref):
    c = cond_ref[...]
    a = c * _sigmoid(c)
    ah, al = _split_bf16(a)
    wh, wl = _split_bf16(w_ref[0])
    o_ref[0] = _mm(ah, wh) + _mm(ah, wl) + _mm(al, wh) + b_ref[0]


def _ada(cond, w_ada, b_ada):
    depth, d, n = w_ada.shape
    tn = 1536
    return pl.pallas_call(
        _ada_kernel,
        grid=(depth, n // tn),
        in_specs=[pl.BlockSpec((COND_ROWS, d), lambda l, j: (0, 0)),
                  pl.BlockSpec((1, d, tn), lambda l, j: (l, 0, j)),
                  pl.BlockSpec((1, 1, tn), lambda l, j: (l, 0, j))],
        out_specs=pl.BlockSpec((1, COND_ROWS, tn), lambda l, j: (l, 0, j)),
        out_shape=jax.ShapeDtypeStruct((depth, COND_ROWS, n), F32),
        compiler_params=pltpu.CompilerParams(dimension_semantics=("arbitrary", "arbitrary"),
                                             vmem_limit_bytes=VMEM_LIMIT),
        name="ada",
    )(cond, w_ada, b_ada.reshape(depth, 1, n))


def _load_x(x_refs, t, pt):
    if len(x_refs) == 1:
        return x_refs[0][...]
    return jnp.where(t < pt, x_refs[0][...], x_refs[1][...])


def _pre_even_kernel(*refs, n_x, pt):
    x_refs = refs[:n_x]
    (mod_ref, g_ref, w_ref, c_ref, s1_ref, s2_ref,
     q_ref, k_ref, v_ref, u_ref, kf_ref, vf_ref) = refs[n_x:]
    t = pl.program_id(0)
    x = _load_x(x_refs, t, pt)
    mod = mod_ref[0, 0]
    h = _rms(x, g_ref[...]) * (1.0 + mod[1:2]) + mod[0:1]
    proj = _mm(h.astype(BF16), w_ref[...])
    aw = A_WIDTH
    q = proj[:, :aw]
    k = proj[:, aw:2 * aw]
    v = proj[:, 2 * aw:3 * aw]
    bw = (proj.shape[1] - 3 * aw) // 2
    gv = proj[:, 3 * aw:3 * aw + bw]
    gg = proj[:, 3 * aw + bw:]
    v_ref[...] = v.astype(BF16)
    u_ref[...] = (gv * _sigmoid(gg)).astype(BF16)
    scale = A_QK_DIM ** -0.5 * LOG2E

    @pl.when(t < pt)
    def _():
        q_ref[...] = (q * scale).astype(BF16)
        k_ref[...] = k.astype(BF16)
        kf_ref[...] = k
        vf_ref[...] = v

    @pl.when(t >= pt)
    def _():
        reps = aw // LANE
        c = jnp.concatenate([c_ref[...]] * reps, axis=1)
        s1 = jnp.concatenate([s1_ref[...]] * reps, axis=1)
        s2 = jnp.concatenate([s2_ref[...]] * reps, axis=1)

        def rope(z):
            return (z * c + pltpu.roll(z, ROPE_FREQS, 1) * s1
                    + pltpu.roll(z, aw - ROPE_FREQS, 1) * s2)

        q_ref[...] = (rope(q) * scale).astype(BF16)
        k_ref[...] = rope(k).astype(BF16)


def _pre_odd_kernel(*refs, n_x, pt):
    x_refs = refs[:n_x]
    mod_ref, g_ref, w_ref, q_ref, k_ref, v_ref, kf_ref, vf_ref = refs[n_x:]
    t = pl.program_id(0)
    x = _load_x(x_refs, t, pt)
    mod = mod_ref[0, 0]
    h = _rms(x, g_ref[...]) * (1.0 + mod[1:2]) + mod[0:1]
    proj = _mm(h.astype(BF16), w_ref[...])
    d = proj.shape[1] // 3
    q = proj[:, :d]
    k = proj[:, d:2 * d]
    v = proj[:, 2 * d:]
    q_ref[...] = (q * (C_HEAD_DIM ** -0.5 * LOG2E)).astype(BF16)
    k_ref[...] = k.astype(BF16)
    v_ref[...] = v.astype(BF16)

    @pl.when(t < pt)
    def _():
        kf_ref[...] = k
        vf_ref[...] = v


def _x_specs(xs, pt, d):
    if len(xs) == 1:
        return [pl.BlockSpec((TM, d), lambda t: (t, 0))]
    return [pl.BlockSpec((TM, d), lambda t: (jnp.minimum(t, pt - 1), 0)),
            pl.BlockSpec((TM, d), lambda t: (jnp.maximum(t - pt, 0), 0))]


def _pre(xs, mod, g, w, rope, *, layer, even, p_tok, s_seq):
    d = xs[0].shape[1]
    t_tok = sum(x.shape[0] for x in xs) if len(xs) == 2 else xs[0].shape[0]
    nt = t_tok // TM
    pt = p_tok // TM
    st = s_seq // TM
    x_specs = _x_specs(xs, pt, d)
    mod_spec = pl.BlockSpec((1, 1, N_MOD, d),
                            lambda t: (layer, jnp.where(t < pt, 0, 1 + (t - pt) // st), 0, 0))
    g_spec = pl.BlockSpec((1, d), lambda t: (0, 0))
    w_spec = pl.BlockSpec(w.shape, lambda t: (0, 0))
    tok = lambda width: pl.BlockSpec((TM, width), lambda t: (t, 0))
    ptok = lambda width: pl.BlockSpec((TM, width), lambda t: (jnp.minimum(t, pt - 1), 0))
    params = pltpu.CompilerParams(dimension_semantics=("arbitrary",), vmem_limit_bytes=VMEM_LIMIT)
    if even:
        aw = A_WIDTH
        bw = (w.shape[1] - 3 * aw) // 2
        rspec = pl.BlockSpec((TM, LANE), lambda t: (jnp.maximum(t - pt, 0) % st, 0))
        return pl.pallas_call(
            functools.partial(_pre_even_kernel, n_x=len(xs), pt=pt),
            grid=(nt,),
            in_specs=x_specs + [mod_spec, g_spec, w_spec, rspec, rspec, rspec],
            out_specs=[tok(aw), tok(aw), tok(aw), tok(bw), ptok(aw), ptok(aw)],
            out_shape=[jax.ShapeDtypeStruct((t_tok, aw), BF16)] * 3
            + [jax.ShapeDtypeStruct((t_tok, bw), BF16)]
            + [jax.ShapeDtypeStruct((p_tok, aw), F32)] * 2,
            compiler_params=params, name="pre_even",
        )(*xs, mod, g, w, *rope)
    return pl.pallas_call(
        functools.partial(_pre_odd_kernel, n_x=len(xs), pt=pt),
        grid=(nt,),
        in_specs=x_specs + [mod_spec, g_spec, w_spec],
        out_specs=[tok(d), tok(d), tok(d), ptok(d), ptok(d)],
        out_shape=[jax.ShapeDtypeStruct((t_tok, d), BF16)] * 3
        + [jax.ShapeDtypeStruct((p_tok, d), F32)] * 2,
        compiler_params=params, name="pre_odd",
    )(*xs, mod, g, w)


def _rope_tables(n_tokens):
    t = jnp.arange(n_tokens, dtype=I32)
    pos = jnp.stack([t // GRID_W, t % GRID_W], axis=-1).astype(F32)
    freqs = ROPE_BASE ** (-jnp.arange(ROPE_FREQS, dtype=F32) / ROPE_FREQS)
    ang = pos[:, :, None] * freqs
    cos, sin = jnp.cos(ang), jnp.sin(ang)
    zero = jnp.zeros_like(sin)
    c = jnp.concatenate([cos, cos], axis=-1).reshape(n_tokens, A_QK_DIM)
    s1 = jnp.concatenate([zero, sin], axis=-1).reshape(n_tokens, A_QK_DIM)
    s2 = jnp.concatenate([-sin, zero], axis=-1).reshape(n_tokens, A_QK_DIM)
    rep = LANE // A_QK_DIM
    return tuple(jnp.tile(a, (1, rep)) for a in (c, s1, s2))


def _stack_maps(q):
    lane = lax.broadcasted_iota(I32, q.shape, 1)
    zero = jnp.zeros_like(q)
    half = LANE // 2
    return jnp.concatenate([jnp.where(lane < half, q, zero), jnp.where(lane >= half, q, zero)], axis=0)


def _pair_attn_kernel(*refs, diff, ctx, groups, tq, nk, tk, lam_init):
    refs = list(refs)
    q_ref = refs.pop(0)
    kc_ref, vc_ref = (refs.pop(0), refs.pop(0)) if ctx else (None, None)
    k_ref, v_ref = refs.pop(0), refs.pop(0)
    lam_ref, sg_ref = (refs.pop(0), refs.pop(0)) if diff else (None, None)
    o_ref = refs.pop(0)
    half = LANE // 2
    for g in range(groups):
        qs = _stack_maps(q_ref[g * tq:(g + 1) * tq, :])
        chunks = []
        if ctx:
            for j in range(kc_ref.shape[2] // tk):
                rows = slice(j * tk, (j + 1) * tk)
                chunks.append((lambda rows=rows: kc_ref[0, 0, rows, :].astype(BF16),
                               lambda rows=rows: vc_ref[0, 0, rows, :].astype(BF16)))
        for j in range(nk):
            rows = slice((g * nk + j) * tk, (g * nk + j + 1) * tk)
            chunks.append((lambda rows=rows: k_ref[rows, :], lambda rows=rows: v_ref[rows, :]))
        ones = jnp.ones((tk, LANE), BF16)
        m = acc = None
        for j, (get_k, get_v) in enumerate(chunks):
            s = _nt(qs, get_k())
            mc = jnp.max(s, axis=-1, keepdims=True)
            vx = jnp.concatenate([get_v(), ones], axis=1)
            if j == 0:
                m = mc
                acc = _mm(jnp.exp2(s - m).astype(BF16), vx)
            else:
                m_new = jnp.maximum(m, mc)
                alpha = jnp.exp2(m - m_new)
                acc = alpha * acc + _mm(jnp.exp2(s - m_new).astype(BF16), vx)
                m = m_new
        o = acc[:, :LANE] / acc[:, LANE:]
        o1 = o[0:tq]
        o2 = o[tq:]
        if diff:
            lf = lam_ref[...]
            lam = (jnp.exp(jnp.sum(lf[0:1] * lf[1:2], axis=1, keepdims=True))
                   - jnp.exp(jnp.sum(lf[2:3] * lf[3:4], axis=1, keepdims=True)) + lam_init)
            dlt = o1 - lam * o2
            out = _rms(dlt, sg_ref[...]) * (1.0 - lam_init)
        else:
            lane = lax.broadcasted_iota(I32, o1.shape, 1)
            out = jnp.where(lane < half, o1, o2)
        o_ref[g * tq:(g + 1) * tq, :] = out.astype(o_ref.dtype)


def _pair_attn(q, k, v, *, batch, n_pairs, nq_tok, nk_tok, q_row0, k_row0, tq, tk, groups,
               ctx=None, diff=False, lam=None, subln=None, lam_init=0.0):
    nk = nk_tok // tk
    if groups > 1:
        assert nq_tok == tq and batch % groups == 0 and ctx is None
        nb, nq = batch // groups, 1
    else:
        nb, nq = batch, nq_tok // tq
    qblk, kblk = groups * tq, groups * nk_tok
    qb0, kb0 = q_row0 // qblk, k_row0 // kblk
    in_specs = [pl.BlockSpec((qblk, LANE), lambda b, h, qi: (qb0 + b * nq + qi, h))]
    args = [q]
    if ctx is not None:
        ck, cv, cj = ctx
        past = ck.shape[2]
        assert past % tk == 0
        in_specs += [pl.BlockSpec((1, 1, past, LANE), lambda b, h, qi: (b, cj, 0, h))] * 2
        args += [ck, cv]
    in_specs += [pl.BlockSpec((kblk, LANE), lambda b, h, qi: (kb0 + b, h))] * 2
    args += [k, v]
    if diff:
        in_specs += [pl.BlockSpec(lam.shape, lambda b, h, qi: (0, 0)),
                     pl.BlockSpec(subln.shape, lambda b, h, qi: (0, 0))]
        args += [lam, subln]
    return pl.pallas_call(
        functools.partial(_pair_attn_kernel, diff=diff, ctx=ctx is not None, groups=groups, tq=tq,
                          nk=nk, tk=tk, lam_init=lam_init),
        grid=(nb, n_pairs, nq),
        in_specs=in_specs,
        out_specs=pl.BlockSpec((qblk, LANE), lambda b, h, qi: (b * nq + qi, h)),
        out_shape=jax.ShapeDtypeStruct((batch * nq_tok, n_pairs * LANE), BF16),
        compiler_params=pltpu.CompilerParams(
            dimension_semantics=("arbitrary",) * 3, vmem_limit_bytes=VMEM_LIMIT),
        name="diff_attn" if diff else "pair_attn",
    )(*args)


def _na_kernel(q_ref, k_ref, v_ref, kc_ref, vc_ref, bias_ref, o_ref, *, nrb, n_tok):
    tq = NA_QROWS * GRID_W
    tkw = NA_KROWS * GRID_W
    half = LANE // 2
    kc = kc_ref[0, 0].astype(BF16)
    vc = jnp.concatenate([vc_ref[0, 0].astype(BF16), jnp.ones((kc.shape[0], LANE), BF16)], axis=1)
    ones = jnp.ones((tkw, LANE), BF16)
    for g in range(NA_GROUPS):
        rb = pl.program_id(2) * NA_GROUPS + g
        qs = _stack_maps(q_ref[g * tq:(g + 1) * tq, :])
        start = pl.multiple_of(jnp.clip(rb * tq - tq, 0, n_tok - tkw), tq)
        case = jnp.where(rb == 0, 0, jnp.where(rb == nrb - 1, 2, 1))
        bias = jnp.concatenate([bias_ref[0, case], bias_ref[1, case]], axis=0)
        s_loc = _nt(qs, k_ref[pl.ds(start, tkw), :]) + bias
        s_ctx = _nt(qs, kc)
        m = jnp.maximum(jnp.max(s_loc, axis=-1, keepdims=True), jnp.max(s_ctx, axis=-1, keepdims=True))
        v_loc = jnp.concatenate([v_ref[pl.ds(start, tkw), :], ones], axis=1)
        acc = _mm(jnp.exp2(s_loc - m).astype(BF16), v_loc) + _mm(jnp.exp2(s_ctx - m).astype(BF16), vc)
        o = acc[:, :LANE] / acc[:, LANE:]
        lane = lax.broadcasted_iota(I32, (tq, LANE), 1)
        o_ref[g * tq:(g + 1) * tq, :] = jnp.where(lane < half, o[0:tq], o[tq:]).astype(o_ref.dtype)


def _na_bias(rpb, rows):
    nrb = rows // NA_QROWS
    h = rpb.shape[0]
    nd = 2 * NA_COLS - 1
    c = np.arange(GRID_W)[:, None]
    kc = np.arange(GRID_W)[None, :]
    cs = np.clip(c - NA_COLS // 2, 0, GRID_W - NA_COLS)
    okc = (kc >= cs) & (kc < cs + NA_COLS)
    dc = np.where(okc, kc - c + NA_COLS - 1, -1)
    onehot = (dc.reshape(1, -1) == np.arange(nd)[:, None]).astype(np.float32)
    colmask = np.where(okc, 0.0, NEG).astype(np.float32).reshape(1, 1, GRID_W, GRID_W)
    colb = jnp.dot(rpb.astype(F32).reshape(-1, nd), onehot, precision=lax.Precision.HIGHEST)
    colb = colb.reshape(h, 2 * NA_ROWS - 1, GRID_W, GRID_W) + colmask
    sel = np.zeros((3, NA_QROWS, NA_KROWS, 2 * NA_ROWS - 1), np.float32)
    rowmask = np.full((3, NA_QROWS, NA_KROWS), NEG, np.float32)
    for x, rb in enumerate((0, 1, nrb - 1)):
        start = int(np.clip(NA_QROWS * rb - NA_QROWS, 0, rows - NA_KROWS))
        for i in range(NA_QROWS):
            r = NA_QROWS * rb + i
            rs = int(np.clip(r - NA_ROWS // 2, 0, rows - NA_ROWS))
            for j in range(NA_KROWS):
                kr = start + j
                if rs <= kr < rs + NA_ROWS:
                    sel[x, i, j, kr - r + NA_ROWS - 1] = 1.0
                    rowmask[x, i, j] = 0.0
    tab = jnp.einsum("xijd,hdck->hxicjk", sel, colb, precision=lax.Precision.HIGHEST)
    tab = (tab + rowmask[None, :, :, None, :, None]) * LOG2E
    return tab.reshape(h, 3, NA_QROWS * GRID_W, NA_KROWS * GRID_W)


def _na(q, k, v, kc, vc, cj, bias, *, batch, n_tok, row0):
    d = q.shape[1]
    n_pairs = d // LANE
    tq = NA_QROWS * GRID_W
    nrb = n_tok // tq
    past = kc.shape[2]
    qblk = NA_GROUPS * tq
    ns = nrb // NA_GROUPS
    qb0 = row0 // qblk
    kb0 = row0 // n_tok
    return pl.pallas_call(
        functools.partial(_na_kernel, nrb=nrb, n_tok=n_tok),
        grid=(n_pairs, batch, ns),
        in_specs=[pl.BlockSpec((qblk, LANE), lambda h, b, r: (qb0 + b * ns + r, h)),
                  pl.BlockSpec((n_tok, LANE), lambda h, b, r: (kb0 + b, h)),
                  pl.BlockSpec((n_tok, LANE), lambda h, b, r: (kb0 + b, h)),
                  pl.BlockSpec((1, 1, past, LANE), lambda h, b, r: (b, cj, 0, h)),
                  pl.BlockSpec((1, 1, past, LANE), lambda h, b, r: (b, cj, 0, h)),
                  pl.BlockSpec((2,) + bias.shape[1:], lambda h, b, r: (h, 0, 0, 0))],
        out_specs=pl.BlockSpec((qblk, LANE), lambda h, b, r: (b * ns + r, h)),
        out_shape=jax.ShapeDtypeStruct((batch * n_tok, d), BF16),
        compiler_params=pltpu.CompilerParams(
            dimension_semantics=("arbitrary",) * 3, vmem_limit_bytes=VMEM_LIMIT),
        name="na_attn",
    )(q, k, v, kc, vc, bias)


def _route_kernel(logit_ref, rb_ref, route_ref, cnt_ref):
    parts = [logit_ref[j] for j in range(logit_ref.shape[0])]
    logits = parts[0] if len(parts) == 1 else jnp.concatenate(parts, axis=1)
    scores = _sigmoid(logits)
    sel = scores + rb_ref[...]
    tm = sel.shape[1]
    gs = []
    for g in range(N_GROUPS):
        r = [sel[g * EXPERTS_PER_GROUP + a:g * EXPERTS_PER_GROUP + a + 1] for a in range(EXPERTS_PER_GROUP)]
        best = None
        for a in range(EXPERTS_PER_GROUP):
            for b in range(a + 1, EXPERTS_PER_GROUP):
                pair = r[a] + r[b]
                best = pair if best is None else jnp.maximum(best, pair)
        gs.append(best)
    bg = jnp.zeros((1, tm), I32)
    bs = gs[0]
    for g in range(1, N_GROUPS):
        better = gs[g] > bs
        bg = jnp.where(better, g, bg)
        bs = jnp.where(better, gs[g], bs)
    eidx = lax.broadcasted_iota(I32, sel.shape, 0)
    masked = jnp.where(eidx // EXPERTS_PER_GROUP == bg, sel, -jnp.inf)
    m1 = jnp.max(masked, axis=0, keepdims=True)
    i1 = jnp.min(jnp.where(masked == m1, eidx, N_EXPERTS), axis=0, keepdims=True)
    masked2 = jnp.where(eidx == i1, -jnp.inf, masked)
    m2 = jnp.max(masked2, axis=0, keepdims=True)
    i2 = jnp.min(jnp.where(masked2 == m2, eidx, N_EXPERTS), axis=0, keepdims=True)
    s1 = jnp.sum(jnp.where(eidx == i1, scores, 0.0), axis=0, keepdims=True)
    s2 = jnp.sum(jnp.where(eidx == i2, scores, 0.0), axis=0, keepdims=True)
    tot = s1 + s2
    route_ref[...] = jnp.concatenate(
        [i1.astype(F32), i2.astype(F32), s1 / tot, s2 / tot, jnp.zeros((4, tm), F32)], axis=0)
    chosen = jnp.where(eidx == i1, 1.0, jnp.where(eidx == i2, 1.0, 0.0))
    lane = lax.broadcasted_iota(I32, (N_EXPERTS, LANE), 1)
    cnt = jnp.zeros((N_EXPERTS, LANE), F32)
    for j in range(tm // TD):
        cj = jnp.sum(chosen[:, j * TD:(j + 1) * TD], axis=1, keepdims=True)
        cnt = jnp.where(lane == j, cj, cnt)
    cnt_ref[0] = cnt


def _route(logits, rbias):
    nt = logits.shape[0]
    t_tok = nt * TM
    step = ROUTE_TILES
    while nt % step:
        step //= 2
    span = step * TM
    assert span % TD == 0
    route, cnt = pl.pallas_call(
        _route_kernel,
        grid=(nt // step,),
        in_specs=[pl.BlockSpec((step, N_EXPERTS, TM), lambda i: (i, 0, 0)),
                  pl.BlockSpec(rbias.shape, lambda i: (0, 0))],
        out_specs=[pl.BlockSpec((8, span), lambda i: (0, i)),
                   pl.BlockSpec((1, N_EXPERTS, LANE), lambda i: (i, 0, 0))],
        out_shape=[jax.ShapeDtypeStruct((8, t_tok), F32),
                   jax.ShapeDtypeStruct((nt // step, N_EXPERTS, LANE), F32)],
        compiler_params=pltpu.CompilerParams(dimension_semantics=("arbitrary",),
                                             vmem_limit_bytes=VMEM_LIMIT),
        name="route",
    )(logits, rbias)
    per = span // TD
    cnt = cnt[:, :, :per].astype(I32).transpose(0, 2, 1).reshape(t_tok // TD, N_EXPERTS)
    return route, cnt


def _post_kernel(*refs, even, n_x, pt, st):
    a_refs, refs = refs[:2], refs[2:]
    if even:
        (up_ref, uc_ref, un_ref, cw_ref, cb_ref, cg_ref, cnb_ref, wo_ref) = refs[:8]
        rest = refs[8:]
    else:
        wo_ref = refs[0]
        rest = refs[1:]
    x_refs = rest[:n_x]
    (mod_ref, g2_ref, rw2_ref, xn_ref, h2_ref, logit_ref) = rest[n_x:n_x + 6]
    t = pl.program_id(0)
    a = _load_x(a_refs, t, pt)
    if even:
        ext, shifted = rest[n_x + 6:]
        n_sub = ext.shape[0]
        sub = TM // n_sub
        latent = t >= pt
        js = jnp.maximum(t - pt, 0) % st
        pad = CONV_WIDTH // 2
        span = shifted.shape[2]
        blocks = []
        for s in range(n_sub):
            if s == 0:
                prev, has_prev = up_ref[...], jnp.logical_and(latent, js != 0)
            else:
                prev, has_prev = uc_ref[s * sub - HALO:s * sub, :], latent
            if s == n_sub - 1:
                nxt, has_next = un_ref[...], jnp.logical_and(latent, js != st - 1)
            else:
                nxt, has_next = uc_ref[(s + 1) * sub:(s + 1) * sub + HALO, :], latent
            prev = prev.astype(F32)
            nxt = nxt.astype(F32)
            ext[s, 0:HALO, :] = jnp.where(has_prev, prev, jnp.zeros_like(prev))
            ext[s, HALO:HALO + sub, :] = uc_ref[s * sub:(s + 1) * sub, :].astype(F32)
            ext[s, HALO + sub:, :] = jnp.where(has_next, nxt, jnp.zeros_like(nxt))
            for ph in range(1, SUBLANE):
                shifted[s, ph - 1] = ext[s, pl.ds(ph, span), :]
            for r0 in range(0, sub, CONV_ROWS):
                acc = jnp.zeros((CONV_ROWS, ext.shape[2]), F32) + cb_ref[...]
                for j in range(CONV_WIDTH):
                    whole, ph = divmod(HALO - pad + j, SUBLANE)
                    rows = slice(r0 + whole * SUBLANE, r0 + whole * SUBLANE + CONV_ROWS)
                    tap = ext[s, rows, :] if ph == 0 else shifted[s, ph - 1, rows, :]
                    acc = acc + cw_ref[j:j + 1, :] * tap
                blocks.append(acc)
        acc = jnp.concatenate(blocks, axis=0)
        mu = jnp.mean(acc, axis=-1, keepdims=True)
        var = jnp.mean(jnp.square(acc - mu), axis=-1, keepdims=True)
        y = (acc - mu) * lax.rsqrt(var + EPS) * cg_ref[...] + cnb_ref[...]
        ua = (y * _sigmoid(y)).astype(BF16)
        aw = a.shape[1]
        mix = _mm(a, wo_ref[0:aw, :]) + _mm(ua, wo_ref[aw:, :])
    else:
        mix = _mm(a, wo_ref[...])
    x = _load_x(x_refs, t, pt)
    mod = mod_ref[0, 0]
    xn = x + mod[2:3] * mix
    xn_ref[...] = xn
    h2 = _rms(xn, g2_ref[...]) * (1.0 + mod[4:5]) + mod[3:4]
    h2_ref[...] = h2.astype(BF16)
    hh, hl = _split_bf16(h2)
    both = _mm(hh, rw2_ref[...])
    lt = both[:, :LANE] + both[:, LANE:] + _mm(hl, rw2_ref[:, 0:LANE])
    logit_ref[0] = lt.T[0:N_EXPERTS, :]


def _post(a_parts, u, conv, wo, xs, mod, g2, rw2, *, layer, even, p_tok, p_seq, s_seq):
    d = wo.shape[1]
    t_tok = a_parts[0].shape[0] + a_parts[1].shape[0]
    nt = t_tok // TM
    pt = p_tok // TM
    st = s_seq // TM
    tok = lambda width: pl.BlockSpec((TM, width), lambda t: (t, 0))
    full = lambda arr: pl.BlockSpec(arr.shape, lambda t: (0,) * arr.ndim)
    in_specs = _x_specs(a_parts, pt, a_parts[0].shape[1])
    args = list(a_parts)
    scratch = []
    if even:
        bw = u.shape[1]
        hb = TM // HALO
        in_specs += [pl.BlockSpec((HALO, bw), lambda t: (jnp.maximum(t * hb - 1, 0), 0)),
                     tok(bw),
                     pl.BlockSpec((HALO, bw), lambda t: (jnp.minimum((t + 1) * hb, nt * hb - 1), 0))]
        args += [u, u, u]
        for arr in conv:
            in_specs.append(full(arr))
            args.append(arr)
        in_specs.append(full(wo))
        args.append(wo)
        reach = (HALO + CONV_WIDTH // 2) // SUBLANE * SUBLANE
        n_sub = TM // p_seq
        scratch = [pltpu.VMEM((n_sub, p_seq + 2 * HALO, bw), F32),
                   pltpu.VMEM((n_sub, SUBLANE - 1, p_seq + reach, bw), F32)]
    else:
        in_specs.append(full(wo))
        args.append(wo)
    in_specs += _x_specs(xs, pt, d)
    args += list(xs)
    in_specs += [pl.BlockSpec((1, 1, N_MOD, d),
                              lambda t: (layer, jnp.where(t < pt, 0, 1 + (t - pt) // st), 0, 0)),
                 full(g2), full(rw2)]
    args += [mod, g2, rw2]
    return pl.pallas_call(
        functools.partial(_post_kernel, even=even, n_x=len(xs), pt=pt, st=st),
        grid=(nt,),
        in_specs=in_specs,
        out_specs=[tok(d), tok(d), pl.BlockSpec((1, N_EXPERTS, TM), lambda t: (t, 0, 0))],
        out_shape=[jax.ShapeDtypeStruct((t_tok, d), F32), jax.ShapeDtypeStruct((t_tok, d), BF16),
                   jax.ShapeDtypeStruct((nt, N_EXPERTS, TM), F32)],
        scratch_shapes=scratch,
        compiler_params=pltpu.CompilerParams(dimension_semantics=("arbitrary",),
                                             vmem_limit_bytes=VMEM_LIMIT),
        name="post_even" if even else "post_odd",
    )(*args)


def _stage_rows():
    return 2 * TD + N_EXPERTS * BF16_ROWS


def _chunk_bits():
    bits = []
    b = TD
    while b >= BF16_ROWS:
        bits.append(b)
        b //= 2
    return bits


def _moe_plan(cnt, t_tok):
    cnt16 =(cnt + BF16_ROWS - 1) // BF16_ROWS * BF16_ROWS
    loff = jnp.cumsum(cnt16, axis=1) - cnt16
    tot = cnt16.sum(axis=0)
    reg = (tot + TF - 1) // TF * TF
    off = jnp.cumsum(reg) - reg
    goff = off[None, :] + jnp.cumsum(cnt16, axis=0) - cnt16
    ends = jnp.cumsum(reg // TF)
    n_used = ends[-1]
    n_tiles = _sorted_rows(t_tok) // TF
    tile = jnp.minimum(jnp.arange(n_tiles, dtype=I32), n_used - 1)
    tile_expert = jnp.minimum(jnp.sum(tile[:, None] >= ends[None, :], axis=1), N_EXPERTS - 1).astype(I32)
    tail = jnp.stack([n_used * TF, n_tiles - n_used])
    return dict(cnt16=cnt16.reshape(-1), loff=loff.reshape(-1), goff=goff.reshape(-1),
                gap_start=jnp.concatenate([off + tot, tail[0:1]]).astype(I32),
                gap_len=jnp.concatenate([reg - tot, tail[1:2]]).astype(I32),
                tile=tile.astype(I32), tile_expert=tile_expert,
                n_used=jnp.reshape(n_used, (1,)).astype(I32))


def _sorted_rows(t_tok):
    nd = t_tok // TD
    worst = 2 * t_tok + nd * N_EXPERTS * (BF16_ROWS - 1) + N_EXPERTS * (TF - 1)
    return (worst + TF - 1) // TF * TF


def _perm(route, loff_ref, tile):
    e0 = route[0:1].astype(I32)
    e1 = route[1:2].astype(I32)
    eidx = lax.broadcasted_iota(I32, (N_EXPERTS, TD), 0)
    m0 = eidx == e0
    m1 = eidx == e1
    chosen = jnp.where(m0, 1.0, jnp.where(m1, 1.0, 0.0)).astype(BF16)
    before = (lax.broadcasted_iota(I32, (TD, TD), 0) < lax.broadcasted_iota(I32, (TD, TD), 1))
    rank = _mm(chosen, jnp.where(before, 1.0, 0.0).astype(BF16))
    ecol = lax.broadcasted_iota(I32, (N_EXPERTS, 1), 0)
    lcol = jnp.zeros((N_EXPERTS, 1), F32)
    for e in range(N_EXPERTS):
        lcol = jnp.where(ecol == e, loff_ref[tile * N_EXPERTS + e].astype(F32), lcol)
    base = rank + lcol
    slot0 = jnp.sum(jnp.where(m0, base, 0.0), axis=0, keepdims=True).astype(I32)
    slot1 = jnp.sum(jnp.where(m1, base, 0.0), axis=0, keepdims=True).astype(I32)
    rio = lax.broadcasted_iota(I32, (_stage_rows(), TD), 0)
    return rio == slot0, rio == slot1


def _chunk_copies(cnt_ref, loff_ref, goff_ref, tile, stage, sorted_hbm, sem, *, to_sorted, wait):
    for e in range(N_EXPERTS):
        n = cnt_ref[tile * N_EXPERTS + e]
        lo = loff_ref[tile * N_EXPERTS + e]
        go = goff_ref[tile * N_EXPERTS + e]
        for bit in _chunk_bits():
            done = n & ~(2 * bit - 1)
            s_view = stage.at[pl.ds(pl.multiple_of(lo + done, BF16_ROWS), bit), :]
            h_view = sorted_hbm.at[pl.ds(pl.multiple_of(go + done, BF16_ROWS), bit), :]
            cp = (pltpu.make_async_copy(s_view, h_view, sem) if to_sorted
                  else pltpu.make_async_copy(h_view, s_view, sem))

            @pl.when((n & bit) != 0)
            def _(cp=cp):
                cp.wait() if wait else cp.start()


def _dispatch_kernel(cnt_ref, loff_ref, goff_ref, gs_ref, gl_ref,
                     route_ref, h2_ref, xs_hbm, stage, zeros, sem, zsem):
    i = pl.program_id(0)
    n = pl.num_programs(0)
    slot = i % 2
    p0, p1 = _perm(route_ref[...], loff_ref, i)
    p = jnp.where(p0, 1.0, jnp.where(p1, 1.0, 0.0)).astype(BF16)
    stage[slot] =_mm(p, h2_ref[...]).astype(BF16)
    copies = functools.partial(_chunk_copies, cnt_ref, loff_ref, goff_ref, to_sorted=True)
    copies(i, stage.at[slot], xs_hbm, sem.at[slot], wait=False)

    @pl.when(i > 0)
    def _():
        copies(i - 1, stage.at[1 - slot], xs_hbm, sem.at[1 - slot], wait=True)

    @pl.when(i == n - 1)
    def _():
        zeros[...] = jnp.zeros(zeros.shape, zeros.dtype)
        bits = [b for b in _chunk_bits() if b < TF]
        for wait in (False, True):
            for e in range(N_EXPERTS):
                g0, gl = gs_ref[e], gl_ref[e]
                for bit in bits:
                    done = gl & ~(2 * bit - 1)
                    cp = pltpu.make_async_copy(
                        zeros.at[pl.ds(0, bit), :],
                        xs_hbm.at[pl.ds(pl.multiple_of(g0 + done, BF16_ROWS), bit), :], zsem)

                    @pl.when((gl & bit) != 0)
                    def _(cp=cp, wait=wait):
                        cp.wait() if wait else cp.start()
        tail0, ntail = gs_ref[N_EXPERTS], gl_ref[N_EXPERTS]

        def tail_copy(j):
            return pltpu.make_async_copy(
                zeros, xs_hbm.at[pl.ds(pl.multiple_of(tail0 + j * TF, TF), TF), :], zsem)

        @pl.loop(0, ntail)
        def _(j):
            tail_copy(j).start()

        @pl.loop(0, ntail)
        def _(j):
            tail_copy(j).wait()

        copies(i, stage.at[slot], xs_hbm, sem.at[slot], wait=True)


def _dispatch(plan, route, h2):
    t_tok, d = h2.shape
    nd = t_tok // TD
    rows = _sorted_rows(t_tok)
    grid_spec = pltpu.PrefetchScalarGridSpec(
        num_scalar_prefetch=5,
        grid=(nd,),
        in_specs=[pl.BlockSpec((8, TD), lambda i, *_: (0, i)),
                  pl.BlockSpec((TD, d), lambda i, *_: (i, 0))],
        out_specs=pl.BlockSpec(memory_space=pl.ANY),
        scratch_shapes=[pltpu.VMEM((2, _stage_rows(), d), BF16), pltpu.VMEM((TF, d), BF16),
                        pltpu.SemaphoreType.DMA((2,)), pltpu.SemaphoreType.DMA(())],
    )
    return pl.pallas_call(
        _dispatch_kernel,
        grid_spec=grid_spec,
        out_shape=jax.ShapeDtypeStruct((rows, d), BF16),
        compiler_params=pltpu.CompilerParams(dimension_semantics=("arbitrary",),
                                             vmem_limit_bytes=VMEM_LIMIT),
        name="moe_dispatch",
    )(plan["cnt16"], plan["loff"], plan["goff"], plan["gap_start"], plan["gap_len"], route, h2)


def _ffn_kernel(tile_ref, te_ref, nu_ref, x_ref, w1_ref, w3_ref, w2_ref, y_ref, w1b, w3b, w2b):
    i = pl.program_id(0)

    @pl.when(jnp.logical_or(i == 0, te_ref[i] != te_ref[jnp.maximum(i - 1, 0)]))
    def _():
        w1b[...] = w1_ref[0, 0].astype(BF16)
        w3b[...] = w3_ref[0, 0].astype(BF16)
        w2b[...] = w2_ref[0, 0].astype(BF16)

    @pl.when(i < nu_ref[0])
    def _():
        x = x_ref[...]
        a = _mm(x, w1b[...])
        b = _mm(x, w3b[...])
        he = (a * _sigmoid(a) * b).astype(BF16)
        y_ref[...] = _mm(he, w2b[...]).astype(y_ref.dtype)

    @pl.when(i >= nu_ref[0])
    def _():
        y_ref[...] = jnp.zeros(y_ref.shape, y_ref.dtype)


def _ffn(plan, xs, w1, w3, w2, layer):
    rows, d = xs.shape
    f = w1.shape[3]
    grid_spec = pltpu.PrefetchScalarGridSpec(
        num_scalar_prefetch=3,
        grid=(rows // TF,),
        in_specs=[pl.BlockSpec((TF, d), lambda i, tile, te, nu: (tile[i], 0)),
                  pl.BlockSpec((1, 1, d, f), lambda i, tile, te, nu: (layer, te[i], 0, 0)),
                  pl.BlockSpec((1, 1, d, f), lambda i, tile, te, nu: (layer, te[i], 0, 0)),
                  pl.BlockSpec((1, 1, f, d), lambda i, tile, te, nu: (layer, te[i], 0, 0))],
        out_specs=pl.BlockSpec((TF, d), lambda i, tile, te, nu: (i, 0)),
        scratch_shapes=[pltpu.VMEM((d, f), BF16), pltpu.VMEM((d, f), BF16), pltpu.VMEM((f, d), BF16)],
    )
    return pl.pallas_call(
        _ffn_kernel,
        grid_spec=grid_spec,
        out_shape=jax.ShapeDtypeStruct((rows, d), BF16),
        compiler_params=pltpu.CompilerParams(dimension_semantics=("arbitrary",),
                                             vmem_limit_bytes=VMEM_LIMIT),
        name="moe_ffn",
    )(plan["tile"], plan["tile_expert"], plan["n_used"], xs, w1, w3, w2)


def _combine_kernel(*refs, final, ptd):
    cnt_ref, loff_ref, goff_ref, route_ref, ys_hbm, xn_ref, mod_ref = refs[:7]
    if final:
        fg_ref, yp_ref, ysm_ref, stage, sem = refs[7:]
    else:
        xo_ref, stage, sem = refs[7:]
    i = pl.program_id(0)
    n = pl.num_programs(0)
    slot = i % 2
    copies = functools.partial(_chunk_copies, cnt_ref, loff_ref, goff_ref, to_sorted=False)

    @pl.when(i == 0)
    def _():
        stage[...] = jnp.zeros(stage.shape, stage.dtype)
        copies(i, stage.at[0], ys_hbm, sem.at[0], wait=False)

    @pl.when(i + 1 < n)
    def _():
        copies(i + 1, stage.at[1 - slot], ys_hbm, sem.at[1 - slot], wait=False)

    route = route_ref[...]
    p0, p1 = _perm(route, loff_ref, i)
    gate_rows = jnp.where(p0, route[2:3], jnp.where(p1, route[3:4], 0.0))
    gate_col = jnp.sum(gate_rows, axis=1, keepdims=True)
    p = jnp.where(p0, 1.0, jnp.where(p1, 1.0, 0.0)).astype(BF16)
    copies(i, stage.at[slot], ys_hbm, sem.at[slot], wait=True)
    scaled = (stage[slot].astype(F32) * gate_col).astype(BF16)
    y = lax.dot_general(p, scaled, (((0,), (0,)), ((), ())), preferred_element_type=F32)
    xo = xn_ref[...] + mod_ref[0, 0][5:6] * y
    if final:
        out = _rms(xo, fg_ref[...])

        @pl.when(i < ptd)
        def _():
            yp_ref[...] = out

        @pl.when(i >= ptd)
        def _():
            ysm_ref[...] = out
    else:
        xo_ref[...] = xo


def _combine(plan, route, ys, xn, mod, final_g, *, layer, final, p_tok, s_seq):
    t_tok, d = xn.shape
    nd = t_tok // TD
    ptd = p_tok // TD
    std = s_seq // TD
    in_specs = [pl.BlockSpec((8, TD), lambda i, *_: (0, i)),
                pl.BlockSpec(memory_space=pl.ANY),
                pl.BlockSpec((TD, d), lambda i, *_: (i, 0)),
                pl.BlockSpec((1, 1, N_MOD, d),
                             lambda i, *_: (layer, jnp.where(i < ptd, 0, 1 + (i - ptd) // std), 0, 0))]
    args = [route, ys, xn, mod]
    if final:
        in_specs.append(pl.BlockSpec((1, d), lambda i, *_: (0, 0)))
        args.append(final_g)
        out_specs = [pl.BlockSpec((TD, d), lambda i, *_: (jnp.minimum(i, ptd - 1), 0)),
                     pl.BlockSpec((TD, d), lambda i, *_: (jnp.maximum(i - ptd, 0), 0))]
        out_shape = [jax.ShapeDtypeStruct((p_tok, d), F32),
                     jax.ShapeDtypeStruct((t_tok - p_tok, d), F32)]
    else:
        out_specs = pl.BlockSpec((TD, d), lambda i, *_: (i, 0))
        out_shape = jax.ShapeDtypeStruct((t_tok, d), F32)
    grid_spec = pltpu.PrefetchScalarGridSpec(
        num_scalar_prefetch=3, grid=(nd,), in_specs=in_specs, out_specs=out_specs,
        scratch_shapes=[pltpu.VMEM((2, _stage_rows(), d), BF16), pltpu.SemaphoreType.DMA((2,))])
    return pl.pallas_call(
        functools.partial(_combine_kernel, final=final, ptd=ptd),
        grid_spec=grid_spec, out_shape=out_shape,
        compiler_params=pltpu.CompilerParams(dimension_semantics=("arbitrary",),
                                             vmem_limit_bytes=VMEM_LIMIT),
        name="moe_combine_final" if final else "moe_combine",
    )(plan["cnt16"], plan["loff"], plan["goff"], *args)


def kernel(x_prompt, x_sample, cache_even_k, cache_even_v, cache_odd_k, cache_odd_v, c, c_ctx, w_ada, b_ada, norm_g, final_g, even_w_in, even_w_out, even_lam, even_subln_g, even_conv_w, even_conv_b, even_conv_norm_g, even_conv_norm_b, odd_w_qkv, odd_w_out, odd_rpb, router_w, router_b, moe_w1, moe_w3, moe_w2):
    batch, seq, d = x_prompt.shape
    dbatch, dseq, _ = x_sample.shape
    past = cache_even_k.shape[2]
    depth = w_ada.shape[0]
    p_tok, s_tok = batch * seq, dbatch * dseq
    t_tok = p_tok + s_tok
    assert TM % seq == 0 and seq % CONV_ROWS == 0 and seq >= HALO
    assert dseq % TM == 0 and dseq % TD == 0 and p_tok % TM == 0 and p_tok % TD == 0 and p_tok % dseq == 0
    assert 1 + dbatch <= COND_ROWS and dseq % GRID_W == 0 and past % TD == 0
    assert dseq // GRID_W >= NA_KROWS + NA_QROWS

    cond = jnp.zeros((COND_ROWS, d), F32).at[0].set(c_ctx).at[1:1 + dbatch].set(c)
    mod = _ada(cond, w_ada, b_ada).reshape(depth, COND_ROWS, N_MOD, d)
    rope = _rope_tables(dseq)
    rwh, rwl = _split_bf16(jnp.pad(router_w, ((0, 0), (0, LANE - N_EXPERTS))))
    rw2 = jnp.concatenate([rwh, rwl], axis=1)
    rbias = router_b.reshape(N_EXPERTS, 1)
    xs = [x_prompt.reshape(p_tok, d), x_sample.reshape(s_tok, d)]
    outs = {}

    for layer in range(depth):
        j = layer // 2
        even = layer % 2 == 0
        g1 = norm_g[layer, 0].reshape(1, d)
        g2 = norm_g[layer, 1].reshape(1, d)
        if even:
            q, k, v, u, kf, vf = _pre(xs, mod, g1, even_w_in[j].astype(BF16), rope, layer=layer,
                                      even=True, p_tok=p_tok, s_seq=dseq)
            outs["even_k"] = kf.reshape(batch, 1, seq, A_HEADS, A_V_DIM)
            outs["even_v"] = vf.reshape(batch, 1, seq, A_HEADS, A_V_DIM)
            lam_init = 0.8 - 0.6 * float(np.exp(-0.3 * layer))
            lam = even_lam[j]
            subln = even_subln_g[j].reshape(1, A_V_DIM)
            ap = _pair_attn(q, k, v, batch=batch, n_pairs=A_HEADS, nq_tok=seq, nk_tok=seq, q_row0=0,
                            k_row0=0, tq=seq, tk=seq, groups=PROMPT_GROUPS, diff=True, lam=lam,
                            subln=subln, lam_init=lam_init)
            ctx = (cache_even_k.reshape(dbatch, -1, past, A_WIDTH),
                   cache_even_v.reshape(dbatch, -1, past, A_WIDTH), j)
            a_s = _pair_attn(q, k, v, batch=dbatch, n_pairs=A_HEADS, nq_tok=dseq, nk_tok=dseq,
                             q_row0=p_tok, k_row0=p_tok, tq=ATT_TQ, tk=ATT_TK, groups=1, ctx=ctx, diff=True,
                             lam=lam, subln=subln, lam_init=lam_init)
            conv = [even_conv_w[j], even_conv_b[j].reshape(1, -1), even_conv_norm_g[j].reshape(1, -1),
                    even_conv_norm_b[j].reshape(1, -1)]
            xn, h2, logits = _post((ap, a_s), u, conv, even_w_out[j].astype(BF16), xs, mod, g2, rw2,
                                   layer=layer, even=True, p_tok=p_tok, p_seq=seq, s_seq=dseq)
        else:
            q, k, v, kf, vf = _pre(xs, mod, g1, odd_w_qkv[j].astype(BF16), None, layer=layer,
                                   even=False, p_tok=p_tok, s_seq=dseq)
            outs["odd_k"] = kf.reshape(batch, 1, seq, C_HEADS, C_HEAD_DIM)
            outs["odd_v"] = vf.reshape(batch, 1, seq, C_HEADS, C_HEAD_DIM)
            n_pairs = d // LANE
            op = _pair_attn(q, k, v, batch=batch, n_pairs=n_pairs, nq_tok=seq, nk_tok=seq, q_row0=0,
                            k_row0=0, tq=seq, tk=seq, groups=PROMPT_GROUPS)
            bias = _na_bias(odd_rpb[j], dseq // GRID_W)
            o_s = _na(q, k, v, cache_odd_k.reshape(dbatch, -1, past, d),
                      cache_odd_v.reshape(dbatch, -1, past, d), j, bias,
                      batch=dbatch, n_tok=dseq, row0=p_tok)
            xn, h2, logits = _post((op, o_s), None, None, odd_w_out[j].astype(BF16), xs, mod, g2, rw2,
                                   layer=layer, even=False, p_tok=p_tok, p_seq=seq, s_seq=dseq)
        route, cnt = _route(logits, rbias)
        plan = _moe_plan(cnt, t_tok)
        x_sorted = _dispatch(plan, route, h2)
        y_sorted = _ffn(plan, x_sorted, moe_w1, moe_w3, moe_w2, layer)
        final = layer == depth - 1
        res = _combine(plan, route, y_sorted, xn, mod, final_g.reshape(1, d), layer=layer, final=final,
                       p_tok=p_tok, s_seq=dseq)
        if final:
            y_prompt, y_sample = res
        else:
            xs = [res]

    return (y_prompt.reshape(batch, seq, d), y_sample.reshape(dbatch, dseq, d),
            outs["even_k"], outs["even_v"], outs["odd_k"], outs["odd_v"])
```

```python
import functools

import numpy as np
import jax
import jax.numpy as jnp
from jax import lax
from jax.experimental import pallas as pl
from jax.experimental.pallas import tpu as pltpu

F32 = jnp.float32
BF16 = jnp.bfloat16
I32 = jnp.int32

GRID_W = 64
EPS = 1e-6
N_MOD = 6
A_HEADS = 4
A_QK_DIM = 64
A_V_DIM = 128
A_WIDTH = A_HEADS * A_V_DIM
CONV_WIDTH = 31
ROPE_BASE = 10000.0
ROPE_FREQS = A_QK_DIM // 4
C_HEADS = 16
C_HEAD_DIM = 64
NA_ROWS = 8
NA_COLS = 16
N_EXPERTS = 16
N_GROUPS = 4
EXPERTS_PER_GROUP = N_EXPERTS // N_GROUPS
NEG = -1e30
LOG2E = 1.4426950408889634

LANE = 128
SUBLANE = 8
BF16_ROWS = 16
VMEM_LIMIT = 52 * 1024 * 1024

TM = 512
ATT_TQ = 512
ATT_TK = 512
TD = 512
TF = 512
HALO = 16
CONV_ROWS = 64
ROUTE_TILES = 8
NA_QROWS = 4
NA_KROWS = NA_QROWS + NA_ROWS
NA_GROUPS = 2
PROMPT_GROUPS = 4
COND_ROWS = 16


def _nt(a, b):
    return lax.dot_general(a, b, (((1,), (1,)), ((), ())), preferred_element_type=F32)


def _mm(a, b):
    return jnp.dot(a, b, preferred_element_type=F32)


def _split_bf16(a):
    hi = a.astype(BF16)
    lo = (a - hi.astype(F32)).astype(BF16)
    return hi, lo


def _sigmoid(x):
    return 1.0 / (1.0 + jnp.exp(-x))


def _rms(x, g):
    return x * lax.rsqrt(jnp.mean(x * x, axis=-1, keepdims=True) + EPS) * g


def _ada_kernel(cond_ref, w_ref, b_ref, o_ref):
    c = cond_ref[...]
    a = c * _sigmoid(c)
    ah, al = _split_bf16(a)
    wh, wl = _split_bf16(w_ref[0])
    o_ref[0] = _mm(ah, wh) + _mm(ah, wl) + _mm(al, wh) + b_ref[0]


def _ada(cond, w_ada, b_ada):
    depth, d, n = w_ada.shape
    tn = 1536
    return pl.pallas_call(
        _ada_kernel,
        grid=(depth, n // tn),
        in_specs=[pl.BlockSpec((COND_ROWS, d), lambda l, j: (0, 0)),
                  pl.BlockSpec((1, d, tn), lambda l, j: (l, 0, j)),
                  pl.BlockSpec((1, 1, tn), lambda l, j: (l, 0, j))],
        out_specs=pl.BlockSpec((1, COND_ROWS, tn), lambda l, j: (l, 0, j)),
        out_shape=jax.ShapeDtypeStruct((depth, COND_ROWS, n), F32),
        compiler_params=pltpu.CompilerParams(dimension_semantics=("arbitrary", "arbitrary"),
                                             vmem_limit_bytes=VMEM_LIMIT),
        name="ada",
    )(cond, w_ada, b_ada.reshape(depth, 1, n))


def _load_x(x_refs, t, pt):
    if len(x_refs) == 1:
        return x_refs[0][...]
    return jnp.where(t < pt, x_refs[0][...], x_refs[1][...])


def _store_heads(ref, val):
    if len(ref.shape) == 2:
        ref[...] = val
        return
    nb, _, seq, heads, hd = ref.shape
    for b in range(nb):
        for h in range(heads):
            ref[b, 0, :, h, :] = val[b * seq:(b + 1) * seq, h * hd:(h + 1) * hd]


def _pre_even_kernel(*refs, n_x, pt):
    x_refs = refs[:n_x]
    (mod_ref, g_ref, w_ref, c_ref, s1_ref, s2_ref,
     q_ref, k_ref, v_ref, u_ref, kf_ref, vf_ref) = refs[n_x:]
    t = pl.program_id(0)
    x = _load_x(x_refs, t, pt)
    mod = mod_ref[0, 0]
    h = _rms(x, g_ref[...]) * (1.0 + mod[1:2]) + mod[0:1]
    proj = _mm(h.astype(BF16), w_ref[...])
    aw = A_WIDTH
    q = proj[:, :aw]
    k = proj[:, aw:2 * aw]
    v = proj[:, 2 * aw:3 * aw]
    bw = (proj.shape[1] - 3 * aw) // 2
    gv = proj[:, 3 * aw:3 * aw + bw]
    gg = proj[:, 3 * aw + bw:]
    v_ref[...] = v.astype(BF16)
    u_ref[...] = (gv * _sigmoid(gg)).astype(BF16)
    scale = A_QK_DIM ** -0.5 * LOG2E

    @pl.when(t < pt)
    def _():
        q_ref[...] = (q * scale).astype(BF16)
        k_ref[...] = k.astype(BF16)
        _store_heads(kf_ref, k)
        _store_heads(vf_ref, v)

    @pl.when(t >= pt)
    def _():
        reps = aw // LANE
        c = jnp.concatenate([c_ref[...]] * reps, axis=1)
        s1 = jnp.concatenate([s1_ref[...]] * reps, axis=1)
        s2 = jnp.concatenate([s2_ref[...]] * reps, axis=1)

        def rope(z):
            return (z * c + pltpu.roll(z, ROPE_FREQS, 1) * s1
                    + pltpu.roll(z, aw - ROPE_FREQS, 1) * s2)

        q_ref[...] = (rope(q) * scale).astype(BF16)
        k_ref[...] = rope(k).astype(BF16)


def _pre_odd_kernel(*refs, n_x, pt):
    x_refs = refs[:n_x]
    mod_ref, g_ref, w_ref, q_ref, k_ref, v_ref, kf_ref, vf_ref = refs[n_x:]
    t = pl.program_id(0)
    x = _load_x(x_refs, t, pt)
    mod = mod_ref[0, 0]
    h = _rms(x, g_ref[...]) * (1.0 + mod[1:2]) + mod[0:1]
    proj = _mm(h.astype(BF16), w_ref[...])
    d = proj.shape[1] // 3
    q = proj[:, :d]
    k = proj[:, d:2 * d]
    v = proj[:, 2 * d:]
    q_ref[...] = (q * (C_HEAD_DIM ** -0.5 * LOG2E)).astype(BF16)
    k_ref[...] = k.astype(BF16)
    v_ref[...] = v.astype(BF16)

    @pl.when(t < pt)
    def _():
        _store_heads(kf_ref, k)
        _store_heads(vf_ref, v)


def _x_specs(xs, pt, d):
    if len(xs) == 1:
        return [pl.BlockSpec((TM, d), lambda t: (t, 0))]
    return [pl.BlockSpec((TM, d), lambda t: (jnp.minimum(t, pt - 1), 0)),
            pl.BlockSpec((TM, d), lambda t: (jnp.maximum(t - pt, 0), 0))]


def _pre(xs, mod, g, w, rope, *, layer, even, p_tok, p_seq, s_seq, heads):
    d = xs[0].shape[1]
    nb = TM // p_seq
    if heads[1] % LANE == 0:
        cache_spec = pl.BlockSpec((nb, 1, p_seq) + heads, lambda t: (jnp.minimum(t, pt - 1), 0, 0, 0, 0))
        cache_shape = jax.ShapeDtypeStruct((p_tok // p_seq, 1, p_seq) + heads, F32)
    else:
        cache_spec = pl.BlockSpec((TM, d), lambda t: (jnp.minimum(t, pt - 1), 0))
        cache_shape = jax.ShapeDtypeStruct((p_tok, d), F32)
    t_tok = sum(x.shape[0] for x in xs) if len(xs) == 2 else xs[0].shape[0]
    nt = t_tok // TM
    pt = p_tok // TM
    st = s_seq // TM
    x_specs = _x_specs(xs, pt, d)
    mod_spec = pl.BlockSpec((1, 1, N_MOD, d),
                            lambda t: (layer, jnp.where(t < pt, 0, 1 + (t - pt) // st), 0, 0))
    g_spec = pl.BlockSpec((1, d), lambda t: (0, 0))
    w_spec = pl.BlockSpec(w.shape, lambda t: (0, 0))
    tok = lambda width: pl.BlockSpec((TM, width), lambda t: (t, 0))
    params = pltpu.CompilerParams(dimension_semantics=("arbitrary",), vmem_limit_bytes=VMEM_LIMIT)
    if even:
        aw = A_WIDTH
        bw = (w.shape[1] - 3 * aw) // 2
        rspec = pl.BlockSpec((TM, LANE), lambda t: (jnp.maximum(t - pt, 0) % st, 0))
        return pl.pallas_call(
            functools.partial(_pre_even_kernel, n_x=len(xs), pt=pt),
            grid=(nt,),
            in_specs=x_specs + [mod_spec, g_spec, w_spec, rspec, rspec, rspec],
            out_specs=[tok(aw), tok(aw), tok(aw), tok(bw), cache_spec, cache_spec],
            out_shape=[jax.ShapeDtypeStruct((t_tok, aw), BF16)] * 3
            + [jax.ShapeDtypeStruct((t_tok, bw), BF16)] + [cache_shape] * 2,
            compiler_params=params, name="pre_even",
        )(*xs, mod, g, w, *rope)
    return pl.pallas_call(
        functools.partial(_pre_odd_kernel, n_x=len(xs), pt=pt),
        grid=(nt,),
        in_specs=x_specs + [mod_spec, g_spec, w_spec],
        out_specs=[tok(d), tok(d), tok(d), cache_spec, cache_spec],
        out_shape=[jax.ShapeDtypeStruct((t_tok, d), BF16)] * 3 + [cache_shape] * 2,
        compiler_params=params, name="pre_odd",
    )(*xs, mod, g, w)


def _rope_tables(n_tokens):
    t = jnp.arange(n_tokens, dtype=I32)
    pos = jnp.stack([t // GRID_W, t % GRID_W], axis=-1).astype(F32)
    freqs = ROPE_BASE ** (-jnp.arange(ROPE_FREQS, dtype=F32) / ROPE_FREQS)
    ang = pos[:, :, None] * freqs
    cos, sin = jnp.cos(ang), jnp.sin(ang)
    zero = jnp.zeros_like(sin)
    c = jnp.concatenate([cos, cos], axis=-1).reshape(n_tokens, A_QK_DIM)
    s1 = jnp.concatenate([zero, sin], axis=-1).reshape(n_tokens, A_QK_DIM)
    s2 = jnp.concatenate([-sin, zero], axis=-1).reshape(n_tokens, A_QK_DIM)
    rep = LANE // A_QK_DIM
    return tuple(jnp.tile(a, (1, rep)) for a in (c, s1, s2))


def _stack_maps(q):
    lane = lax.broadcasted_iota(I32, q.shape, 1)
    zero = jnp.zeros_like(q)
    half = LANE // 2
    return jnp.concatenate([jnp.where(lane < half, q, zero), jnp.where(lane >= half, q, zero)], axis=0)


def _pair_attn_kernel(*refs, diff, ctx, groups, tq, nk, tk, lam_init):
    refs = list(refs)
    q_ref = refs.pop(0)
    kc_ref, vc_ref = (refs.pop(0), refs.pop(0)) if ctx else (None, None)
    k_ref, v_ref = refs.pop(0), refs.pop(0)
    lam_ref, sg_ref = (refs.pop(0), refs.pop(0)) if diff else (None, None)
    o_ref = refs.pop(0)
    half = LANE // 2
    for g in range(groups):
        qs = _stack_maps(q_ref[g * tq:(g + 1) * tq, :])
        chunks = []
        if ctx:
            for j in range(kc_ref.shape[2] // tk):
                rows = slice(j * tk, (j + 1) * tk)
                chunks.append((lambda rows=rows: kc_ref[0, 0, rows, :].astype(BF16),
                               lambda rows=rows: vc_ref[0, 0, rows, :].astype(BF16)))
        for j in range(nk):
            rows = slice((g * nk + j) * tk, (g * nk + j + 1) * tk)
            chunks.append((lambda rows=rows: k_ref[rows, :], lambda rows=rows: v_ref[rows, :]))
        ones = jnp.ones((tk, LANE), BF16)
        m = acc = None
        for j, (get_k, get_v) in enumerate(chunks):
            s = _nt(qs, get_k())
            mc = jnp.max(s, axis=-1, keepdims=True)
            vx = jnp.concatenate([get_v(), ones], axis=1)
            if j == 0:
                m = mc
                acc = _mm(jnp.exp2(s - m).astype(BF16), vx)
            else:
                m_new = jnp.maximum(m, mc)
                alpha = jnp.exp2(m - m_new)
                acc = alpha * acc + _mm(jnp.exp2(s - m_new).astype(BF16), vx)
                m = m_new
        o = acc[:, :LANE] / acc[:, LANE:]
        o1 = o[0:tq]
        o2 = o[tq:]
        if diff:
            lf = lam_ref[...]
            lam = (jnp.exp(jnp.sum(lf[0:1] * lf[1:2], axis=1, keepdims=True))
                   - jnp.exp(jnp.sum(lf[2:3] * lf[3:4], axis=1, keepdims=True)) + lam_init)
            dlt = o1 - lam * o2
            out = _rms(dlt, sg_ref[...]) * (1.0 - lam_init)
        else:
            lane = lax.broadcasted_iota(I32, o1.shape, 1)
            out = jnp.where(lane < half, o1, o2)
        o_ref[g * tq:(g + 1) * tq, :] = out.astype(o_ref.dtype)


def _pair_attn(q, k, v, *, batch, n_pairs, nq_tok, nk_tok, q_row0, k_row0, tq, tk, groups,
               ctx=None, diff=False, lam=None, subln=None, lam_init=0.0):
    nk = nk_tok // tk
    if groups > 1:
        assert nq_tok == tq and batch % groups == 0 and ctx is None
        nb, nq = batch // groups, 1
    else:
        nb, nq = batch, nq_tok // tq
    qblk, kblk = groups * tq, groups * nk_tok
    qb0, kb0 = q_row0 // qblk, k_row0 // kblk
    in_specs = [pl.BlockSpec((qblk, LANE), lambda b, h, qi: (qb0 + b * nq + qi, h))]
    args = [q]
    if ctx is not None:
        ck, cv, cj = ctx
        past = ck.shape[2]
        assert past % tk == 0
        in_specs += [pl.BlockSpec((1, 1, past, LANE), lambda b, h, qi: (b, cj, 0, h))] * 2
        args += [ck, cv]
    in_specs += [pl.BlockSpec((kblk, LANE), lambda b, h, qi: (kb0 + b, h))] * 2
    args += [k, v]
    if diff:
        in_specs += [pl.BlockSpec(lam.shape, lambda b, h, qi: (0, 0)),
                     pl.BlockSpec(subln.shape, lambda b, h, qi: (0, 0))]
        args += [lam, subln]
    return pl.pallas_call(
        functools.partial(_pair_attn_kernel, diff=diff, ctx=ctx is not None, groups=groups, tq=tq,
                          nk=nk, tk=tk, lam_init=lam_init),
        grid=(nb, n_pairs, nq),
        in_specs=in_specs,
        out_specs=pl.BlockSpec((qblk, LANE), lambda b, h, qi: (b * nq + qi, h)),
        out_shape=jax.ShapeDtypeStruct((batch * nq_tok, n_pairs * LANE), BF16),
        compiler_params=pltpu.CompilerParams(
            dimension_semantics=("arbitrary",) * 3, vmem_limit_bytes=VMEM_LIMIT),
        name="diff_attn" if diff else "pair_attn",
    )(*args)


def _na_kernel(q_ref, k_ref, v_ref, kc_ref, vc_ref, colb_ref, o_ref, bias_ref, *, nrb, n_tok):
    tq = NA_QROWS * GRID_W
    tkw = NA_KROWS * GRID_W
    half = LANE // 2
    rows = n_tok // GRID_W

    @pl.when(jnp.logical_and(pl.program_id(1) == 0, pl.program_id(2) == 0))
    def _():
        lane = lax.broadcasted_iota(I32, (GRID_W, LANE), 1)
        masked = jnp.full((GRID_W, LANE), NEG * LOG2E, F32)
        for x, rb in enumerate((0, 1, nrb - 1)):
            start = min(max(NA_QROWS * rb - NA_QROWS, 0), rows - NA_KROWS)
            for hh in range(2):
                for i in range(NA_QROWS):
                    r = NA_QROWS * rb + i
                    rs = min(max(r - NA_ROWS // 2, 0), rows - NA_ROWS)
                    row0 = (hh * NA_QROWS + i) * GRID_W
                    for jp in range(NA_KROWS // 2):
                        kr = start + 2 * jp
                        ok_l = rs <= kr < rs + NA_ROWS
                        ok_r = rs <= kr + 1 < rs + NA_ROWS
                        if ok_l or ok_r:
                            blk = colb_ref[hh, kr - r + NA_ROWS]
                            if not ok_l:
                                blk = jnp.where(lane >= half, blk, masked)
                            if not ok_r:
                                blk = jnp.where(lane < half, blk, masked)
                        else:
                            blk = masked
                        bias_ref[x, row0:row0 + GRID_W, jp * LANE:(jp + 1) * LANE] = blk

    kc = kc_ref[0, 0].astype(BF16)
    vc = jnp.concatenate([vc_ref[0, 0].astype(BF16), jnp.ones((kc.shape[0], LANE), BF16)], axis=1)
    ones = jnp.ones((tkw, LANE), BF16)
    for g in range(NA_GROUPS):
        rb = pl.program_id(2) * NA_GROUPS + g
        qs = _stack_maps(q_ref[g * tq:(g + 1) * tq, :])
        start = pl.multiple_of(jnp.clip(rb * tq - tq, 0, n_tok - tkw), tq)
        case = jnp.where(rb == 0, 0, jnp.where(rb == nrb - 1, 2, 1))
        s_loc = _nt(qs, k_ref[pl.ds(start, tkw), :]) + bias_ref[case]
        s_ctx = _nt(qs, kc)
        m = jnp.maximum(jnp.max(s_loc, axis=-1, keepdims=True), jnp.max(s_ctx, axis=-1, keepdims=True))
        v_loc = jnp.concatenate([v_ref[pl.ds(start, tkw), :], ones], axis=1)
        acc = _mm(jnp.exp2(s_loc - m).astype(BF16), v_loc) + _mm(jnp.exp2(s_ctx - m).astype(BF16), vc)
        o = acc[:, :LANE] / acc[:, LANE:]
        lane = lax.broadcasted_iota(I32, (tq, LANE), 1)
        o_ref[g * tq:(g + 1) * tq, :] = jnp.where(lane < half, o[0:tq], o[tq:]).astype(o_ref.dtype)


def _na_bias(rpb):
    h = rpb.shape[0]
    nd = 2 * NA_COLS - 1
    c = np.arange(GRID_W)[:, None]
    kc = np.arange(GRID_W)[None, :]
    cs = np.clip(c - NA_COLS // 2, 0, GRID_W - NA_COLS)
    okc = (kc >= cs) & (kc < cs + NA_COLS)
    dc = np.where(okc, kc - c + NA_COLS - 1, -1)
    onehot = (dc.reshape(1, -1) == np.arange(nd)[:, None]).astype(np.float32)
    colmask = np.where(okc, 0.0, NEG).astype(np.float32).reshape(1, 1, GRID_W, GRID_W)
    colb = jnp.dot(rpb.astype(F32).reshape(-1, nd), onehot, precision=lax.Precision.HIGHEST)
    colb = (colb.reshape(h, 2 * NA_ROWS - 1, GRID_W, GRID_W) + colmask) * LOG2E
    masked = jnp.full((h, 1, GRID_W, GRID_W), NEG * LOG2E, F32)
    colb = jnp.concatenate([masked, colb, masked], axis=1)
    return jnp.concatenate([colb[:, :-1], colb[:, 1:]], axis=-1)


def _na(q, k, v, kc, vc, cj, colb, *, batch, n_tok, row0):
    d = q.shape[1]
    n_pairs = d // LANE
    tq = NA_QROWS * GRID_W
    nrb = n_tok // tq
    past = kc.shape[2]
    qblk = NA_GROUPS * tq
    ns = nrb // NA_GROUPS
    qb0 = row0 // qblk
    kb0 = row0 // n_tok
    return pl.pallas_call(
        functools.partial(_na_kernel, nrb=nrb, n_tok=n_tok),
        grid=(n_pairs, batch, ns),
        in_specs=[pl.BlockSpec((qblk, LANE), lambda h, b, r: (qb0 + b * ns + r, h)),
                  pl.BlockSpec((n_tok, LANE), lambda h, b, r: (kb0 + b, h)),
                  pl.BlockSpec((n_tok, LANE), lambda h, b, r: (kb0 + b, h)),
                  pl.BlockSpec((1, 1, past, LANE), lambda h, b, r: (b, cj, 0, h)),
                  pl.BlockSpec((1, 1, past, LANE), lambda h, b, r: (b, cj, 0, h)),
                  pl.BlockSpec((2,) + colb.shape[1:], lambda h, b, r: (h, 0, 0, 0))],
        out_specs=pl.BlockSpec((qblk, LANE), lambda h, b, r: (b * ns + r, h)),
        out_shape=jax.ShapeDtypeStruct((batch * n_tok, d), BF16),
        scratch_shapes=[pltpu.VMEM((3, 2 * tq, NA_KROWS * GRID_W), F32)],
        compiler_params=pltpu.CompilerParams(
            dimension_semantics=("arbitrary",) * 3, vmem_limit_bytes=VMEM_LIMIT),
        name="na_attn",
    )(q, k, v, kc, vc, colb)


def _route_kernel(logit_ref, rb_ref, route_ref, cnt_ref):
    parts = [logit_ref[j] for j in range(logit_ref.shape[0])]
    logits = parts[0] if len(parts) == 1 else jnp.concatenate(parts, axis=1)
    scores = _sigmoid(logits)
    sel = scores + rb_ref[...]
    tm = sel.shape[1]
    gs = []
    for g in range(N_GROUPS):
        r = [sel[g * EXPERTS_PER_GROUP + a:g * EXPERTS_PER_GROUP + a + 1] for a in range(EXPERTS_PER_GROUP)]
        best = None
        for a in range(EXPERTS_PER_GROUP):
            for b in range(a + 1, EXPERTS_PER_GROUP):
                pair = r[a] + r[b]
                best = pair if best is None else jnp.maximum(best, pair)
        gs.append(best)
    bg = jnp.zeros((1, tm), I32)
    bs = gs[0]
    for g in range(1, N_GROUPS):
        better = gs[g] > bs
        bg = jnp.where(better, g, bg)
        bs = jnp.where(better, gs[g], bs)
    eidx = lax.broadcasted_iota(I32, sel.shape, 0)
    masked = jnp.where(eidx // EXPERTS_PER_GROUP == bg, sel, -jnp.inf)
    m1 = jnp.max(masked, axis=0, keepdims=True)
    i1 = jnp.min(jnp.where(masked == m1, eidx, N_EXPERTS), axis=0, keepdims=True)
    masked2 = jnp.where(eidx == i1, -jnp.inf, masked)
    m2 = jnp.max(masked2, axis=0, keepdims=True)
    i2 = jnp.min(jnp.where(masked2 == m2, eidx, N_EXPERTS), axis=0, keepdims=True)
    s1 = jnp.sum(jnp.where(eidx == i1, scores, 0.0), axis=0, keepdims=True)
    s2 = jnp.sum(jnp.where(eidx == i2, scores, 0.0), axis=0, keepdims=True)
    tot = s1 + s2
    route_ref[...] = jnp.concatenate(
        [i1.astype(F32), i2.astype(F32), s1 / tot, s2 / tot, jnp.zeros((4, tm), F32)], axis=0)
    chosen = jnp.where(eidx == i1, 1.0, jnp.where(eidx == i2, 1.0, 0.0))
    lane = lax.broadcasted_iota(I32, (N_EXPERTS, LANE), 1)
    cnt = jnp.zeros((N_EXPERTS, LANE), F32)
    for j in range(tm // TD):
        cj = jnp.sum(chosen[:, j * TD:(j + 1) * TD], axis=1, keepdims=True)
        cnt = jnp.where(lane == j, cj, cnt)
    cnt_ref[0] = cnt


def _route(logits, rbias):
    nt = logits.shape[0]
    t_tok = nt * TM
    step = ROUTE_TILES
    while nt % step:
        step //= 2
    span = step * TM
    assert span % TD == 0
    route, cnt = pl.pallas_call(
        _route_kernel,
        grid=(nt // step,),
        in_specs=[pl.BlockSpec((step, N_EXPERTS, TM), lambda i: (i, 0, 0)),
                  pl.BlockSpec(rbias.shape, lambda i: (0, 0))],
        out_specs=[pl.BlockSpec((8, span), lambda i: (0, i)),
                   pl.BlockSpec((1, N_EXPERTS, LANE), lambda i: (i, 0, 0))],
        out_shape=[jax.ShapeDtypeStruct((8, t_tok), F32),
                   jax.ShapeDtypeStruct((nt // step, N_EXPERTS, LANE), F32)],
        compiler_params=pltpu.CompilerParams(dimension_semantics=("arbitrary",),
                                             vmem_limit_bytes=VMEM_LIMIT),
        name="route",
    )(logits, rbias)
    per = span // TD
    cnt = cnt[:, :, :per].astype(I32).transpose(0, 2, 1).reshape(t_tok // TD, N_EXPERTS)
    return route, cnt


def _post_kernel(*refs, even, n_x, pt, st):
    a_refs, refs = refs[:2], refs[2:]
    if even:
        (up_ref, uc_ref, un_ref, cw_ref, cb_ref, cg_ref, cnb_ref, wo_ref) = refs[:8]
        rest = refs[8:]
    else:
        wo_ref = refs[0]
        rest = refs[1:]
    x_refs = rest[:n_x]
    (mod_ref, g2_ref, rw2_ref, xn_ref, h2_ref, logit_ref) = rest[n_x:n_x + 6]
    t = pl.program_id(0)
    a = _load_x(a_refs, t, pt)
    if even:
        ext, shifted = rest[n_x + 6:]
        n_sub = ext.shape[0]
        sub = TM // n_sub
        latent = t >= pt
        js = jnp.maximum(t - pt, 0) % st
        pad = CONV_WIDTH // 2
        span = shifted.shape[2]
        blocks = []
        for s in range(n_sub):
            if s == 0:
                prev, has_prev = up_ref[...], jnp.logical_and(latent, js != 0)
            else:
                prev, has_prev = uc_ref[s * sub - HALO:s * sub, :], latent
            if s == n_sub - 1:
                nxt, has_next = un_ref[...], jnp.logical_and(latent, js != st - 1)
            else:
                nxt, has_next = uc_ref[(s + 1) * sub:(s + 1) * sub + HALO, :], latent
            prev = prev.astype(F32)
            nxt = nxt.astype(F32)
            ext[s, 0:HALO, :] = jnp.where(has_prev, prev, jnp.zeros_like(prev))
            ext[s, HALO:HALO + sub, :] = uc_ref[s * sub:(s + 1) * sub, :].astype(F32)
            ext[s, HALO + sub:, :] = jnp.where(has_next, nxt, jnp.zeros_like(nxt))
            for ph in range(1, SUBLANE):
                shifted[s, ph - 1] = ext[s, pl.ds(ph, span), :]
            for r0 in range(0, sub, CONV_ROWS):
                acc = jnp.zeros((CONV_ROWS, ext.shape[2]), F32) + cb_ref[...]
                for j in range(CONV_WIDTH):
                    whole, ph = divmod(HALO - pad + j, SUBLANE)
                    rows = slice(r0 + whole * SUBLANE, r0 + whole * SUBLANE + CONV_ROWS)
                    tap = ext[s, rows, :] if ph == 0 else shifted[s, ph - 1, rows, :]
                    acc = acc + cw_ref[j:j + 1, :] * tap
                blocks.append(acc)
        acc = jnp.concatenate(blocks, axis=0)
        mu = jnp.mean(acc, axis=-1, keepdims=True)
        var = jnp.mean(jnp.square(acc - mu), axis=-1, keepdims=True)
        y = (acc - mu) * lax.rsqrt(var + EPS) * cg_ref[...] + cnb_ref[...]
        ua = (y * _sigmoid(y)).astype(BF16)
        aw = a.shape[1]
        mix = _mm(a, wo_ref[0:aw, :]) + _mm(ua, wo_ref[aw:, :])
    else:
        mix = _mm(a, wo_ref[...])
    x = _load_x(x_refs, t, pt)
    mod = mod_ref[0, 0]
    xn = x + mod[2:3] * mix
    xn_ref[...] = xn
    h2 = _rms(xn, g2_ref[...]) * (1.0 + mod[4:5]) + mod[3:4]
    h2_ref[...] = h2.astype(BF16)
    hh, hl = _split_bf16(h2)
    both = _mm(hh, rw2_ref[...])
    lt = both[:, :LANE] + both[:, LANE:] + _mm(hl, rw2_ref[:, 0:LANE])
    logit_ref[0] = lt.T[0:N_EXPERTS, :]


def _post(a_parts, u, conv, wo, xs, mod, g2, rw2, *, layer, even, p_tok, p_seq, s_seq):
    d = wo.shape[1]
    t_tok = a_parts[0].shape[0] + a_parts[1].shape[0]
    nt = t_tok // TM
    pt = p_tok // TM
    st = s_seq // TM
    tok = lambda width: pl.BlockSpec((TM, width), lambda t: (t, 0))
    full = lambda arr: pl.BlockSpec(arr.shape, lambda t: (0,) * arr.ndim)
    in_specs = _x_specs(a_parts, pt, a_parts[0].shape[1])
    args = list(a_parts)
    scratch = []
    if even:
        bw = u.shape[1]
        hb = TM // HALO
        in_specs += [pl.BlockSpec((HALO, bw), lambda t: (jnp.maximum(t * hb - 1, 0), 0)),
                     tok(bw),
                     pl.BlockSpec((HALO, bw), lambda t: (jnp.minimum((t + 1) * hb, nt * hb - 1), 0))]
        args += [u, u, u]
        for arr in conv:
            in_specs.append(full(arr))
            args.append(arr)
        in_specs.append(full(wo))
        args.append(wo)
        reach = (HALO + CONV_WIDTH // 2) // SUBLANE * SUBLANE
        n_sub = TM // p_seq
        scratch = [pltpu.VMEM((n_sub, p_seq + 2 * HALO, bw), F32),
                   pltpu.VMEM((n_sub, SUBLANE - 1, p_seq + reach, bw), F32)]
    else:
        in_specs.append(full(wo))
        args.append(wo)
    in_specs += _x_specs(xs, pt, d)
    args += list(xs)
    in_specs += [pl.BlockSpec((1, 1, N_MOD, d),
                              lambda t: (layer, jnp.where(t < pt, 0, 1 + (t - pt) // st), 0, 0)),
                 full(g2), full(rw2)]
    args += [mod, g2, rw2]
    return pl.pallas_call(
        functools.partial(_post_kernel, even=even, n_x=len(xs), pt=pt, st=st),
        grid=(nt,),
        in_specs=in_specs,
        out_specs=[tok(d), tok(d), pl.BlockSpec((1, N_EXPERTS, TM), lambda t: (t, 0, 0))],
        out_shape=[jax.ShapeDtypeStruct((t_tok, d), F32), jax.ShapeDtypeStruct((t_tok, d), BF16),
                   jax.ShapeDtypeStruct((nt, N_EXPERTS, TM), F32)],
        scratch_shapes=scratch,
        compiler_params=pltpu.CompilerParams(dimension_semantics=("arbitrary",),
                                             vmem_limit_bytes=VMEM_LIMIT),
        name="post_even" if even else "post_odd",
    )(*args)


def _stage_rows():
    return 2 * TD + N_EXPERTS * BF16_ROWS


def _chunk_bits():
    bits = []
    b = TD
    while b >= BF16_ROWS:
        bits.append(b)
        b //= 2
    return bits


def _moe_plan(cnt, t_tok):
    cnt16 =(cnt + BF16_ROWS - 1) // BF16_ROWS * BF16_ROWS
    loff = jnp.cumsum(cnt16, axis=1) - cnt16
    tot = cnt16.sum(axis=0)
    reg = (tot + TF - 1) // TF * TF
    off = jnp.cumsum(reg) - reg
    goff = off[None, :] + jnp.cumsum(cnt16, axis=0) - cnt16
    ends = jnp.cumsum(reg // TF)
    n_used = ends[-1]
    n_tiles = _sorted_rows(t_tok) // TF
    tile = jnp.minimum(jnp.arange(n_tiles, dtype=I32), n_used - 1)
    tile_expert = jnp.minimum(jnp.sum(tile[:, None] >= ends[None, :], axis=1), N_EXPERTS - 1).astype(I32)
    tail = jnp.stack([n_used * TF, n_tiles - n_used])
    return dict(cnt16=cnt16.reshape(-1), loff=loff.reshape(-1), goff=goff.reshape(-1),
                gap_start=jnp.concatenate([off + tot, tail[0:1]]).astype(I32),
                gap_len=jnp.concatenate([reg - tot, tail[1:2]]).astype(I32),
                tile=tile.astype(I32), tile_expert=tile_expert,
                n_used=jnp.reshape(n_used, (1,)).astype(I32))


def _sorted_rows(t_tok):
    nd = t_tok // TD
    worst = 2 * t_tok + nd * N_EXPERTS * (BF16_ROWS - 1) + N_EXPERTS * (TF - 1)
    return (worst + TF - 1) // TF * TF


def _perm(route, loff_ref, tile):
    e0 = route[0:1].astype(I32)
    e1 = route[1:2].astype(I32)
    eidx = lax.broadcasted_iota(I32, (N_EXPERTS, TD), 0)
    m0 = eidx == e0
    m1 = eidx == e1
    chosen = jnp.where(m0, 1.0, jnp.where(m1, 1.0, 0.0)).astype(BF16)
    before = (lax.broadcasted_iota(I32, (TD, TD), 0) < lax.broadcasted_iota(I32, (TD, TD), 1))
    rank = _mm(chosen, jnp.where(before, 1.0, 0.0).astype(BF16))
    ecol = lax.broadcasted_iota(I32, (N_EXPERTS, 1), 0)
    lcol = jnp.zeros((N_EXPERTS, 1), F32)
    for e in range(N_EXPERTS):
        lcol = jnp.where(ecol == e, loff_ref[tile * N_EXPERTS + e].astype(F32), lcol)
    base = rank + lcol
    slot0 = jnp.sum(jnp.where(m0, base, 0.0), axis=0, keepdims=True).astype(I32)
    slot1 = jnp.sum(jnp.where(m1, base, 0.0), axis=0, keepdims=True).astype(I32)
    rio = lax.broadcasted_iota(I32, (_stage_rows(), TD), 0)
    return rio == slot0, rio == slot1


def _chunk_copies(cnt_ref, loff_ref, goff_ref, tile, stage, sorted_hbm, sem, *, to_sorted, wait):
    if wait:
        total = cnt_ref[tile * N_EXPERTS]
        for e in range(1, N_EXPERTS):
            total = total + cnt_ref[tile * N_EXPERTS + e]
        bit = 2 * TD
        while bit >= BF16_ROWS:
            s_view = stage.at[pl.ds(0, bit), :]
            h_view = sorted_hbm.at[pl.ds(0, bit), :]
            cp = (pltpu.make_async_copy(s_view, h_view, sem) if to_sorted
                  else pltpu.make_async_copy(h_view, s_view, sem))

            @pl.when((total & bit) != 0)
            def _(cp=cp):
                cp.wait()

            bit //= 2
        return
    for e in range(N_EXPERTS):
        n = cnt_ref[tile * N_EXPERTS + e]
        lo = loff_ref[tile * N_EXPERTS + e]
        go = goff_ref[tile * N_EXPERTS + e]
        for bit in _chunk_bits():
            done = n & ~(2 * bit - 1)
            s_view = stage.at[pl.ds(pl.multiple_of(lo + done, BF16_ROWS), bit), :]
            h_view = sorted_hbm.at[pl.ds(pl.multiple_of(go + done, BF16_ROWS), bit), :]
            cp = (pltpu.make_async_copy(s_view, h_view, sem) if to_sorted
                  else pltpu.make_async_copy(h_view, s_view, sem))

            @pl.when((n & bit) != 0)
            def _(cp=cp):
                cp.wait() if wait else cp.start()


def _dispatch_kernel(cnt_ref, loff_ref, goff_ref, gs_ref, gl_ref,
                     route_ref, h2_ref, xs_hbm, stage, zeros, sem, zsem):
    i = pl.program_id(0)
    n = pl.num_programs(0)
    slot = i % 2
    p0, p1 = _perm(route_ref[...], loff_ref, i)
    p = jnp.where(p0, 1.0, jnp.where(p1, 1.0, 0.0)).astype(BF16)
    stage[slot] =_mm(p, h2_ref[...]).astype(BF16)
    copies = functools.partial(_chunk_copies, cnt_ref, loff_ref, goff_ref, to_sorted=True)
    copies(i, stage.at[slot], xs_hbm, sem.at[slot], wait=False)

    @pl.when(i > 0)
    def _():
        copies(i - 1, stage.at[1 - slot], xs_hbm, sem.at[1 - slot], wait=True)

    @pl.when(i == n - 1)
    def _():
        zeros[...] = jnp.zeros(zeros.shape, zeros.dtype)
        bits = [b for b in _chunk_bits() if b < TF]
        for wait in (False, True):
            for e in range(N_EXPERTS):
                g0, gl = gs_ref[e], gl_ref[e]
                for bit in bits:
                    done = gl & ~(2 * bit - 1)
                    cp = pltpu.make_async_copy(
                        zeros.at[pl.ds(0, bit), :],
                        xs_hbm.at[pl.ds(pl.multiple_of(g0 + done, BF16_ROWS), bit), :], zsem)

                    @pl.when((gl & bit) != 0)
                    def _(cp=cp, wait=wait):
                        cp.wait() if wait else cp.start()
        tail0, ntail = gs_ref[N_EXPERTS], gl_ref[N_EXPERTS]

        def tail_copy(j):
            return pltpu.make_async_copy(
                zeros, xs_hbm.at[pl.ds(pl.multiple_of(tail0 + j * TF, TF), TF), :], zsem)

        @pl.loop(0, ntail)
        def _(j):
            tail_copy(j).start()

        @pl.loop(0, ntail)
        def _(j):
            tail_copy(j).wait()

        copies(i, stage.at[slot], xs_hbm, sem.at[slot], wait=True)


def _dispatch(plan, route, h2):
    t_tok, d = h2.shape
    nd = t_tok // TD
    rows = _sorted_rows(t_tok)
    grid_spec = pltpu.PrefetchScalarGridSpec(
        num_scalar_prefetch=5,
        grid=(nd,),
        in_specs=[pl.BlockSpec((8, TD), lambda i, *_: (0, i)),
                  pl.BlockSpec((TD, d), lambda i, *_: (i, 0))],
        out_specs=pl.BlockSpec(memory_space=pl.ANY),
        scratch_shapes=[pltpu.VMEM((2, _stage_rows(), d), BF16), pltpu.VMEM((TF, d), BF16),
                        pltpu.SemaphoreType.DMA((2,)), pltpu.SemaphoreType.DMA(())],
    )
    return pl.pallas_call(
        _dispatch_kernel,
        grid_spec=grid_spec,
        out_shape=jax.ShapeDtypeStruct((rows, d), BF16),
        compiler_params=pltpu.CompilerParams(dimension_semantics=("arbitrary",),
                                             vmem_limit_bytes=VMEM_LIMIT),
        name="moe_dispatch",
    )(plan["cnt16"], plan["loff"], plan["goff"], plan["gap_start"], plan["gap_len"], route, h2)


def _ffn_kernel(tile_ref, te_ref, nu_ref, x_ref, w1_ref, w3_ref, w2_ref, y_ref, w1b, w3b, w2b):
    i = pl.program_id(0)

    @pl.when(jnp.logical_or(i == 0, te_ref[i] != te_ref[jnp.maximum(i - 1, 0)]))
    def _():
        w1b[...] = w1_ref[0, 0].astype(BF16)
        w3b[...] = w3_ref[0, 0].astype(BF16)
        w2b[...] = w2_ref[0, 0].astype(BF16)

    @pl.when(i < nu_ref[0])
    def _():
        x = x_ref[...]
        a = _mm(x, w1b[...])
        b = _mm(x, w3b[...])
        he = (a * _sigmoid(a) * b).astype(BF16)
        y_ref[...] = _mm(he, w2b[...]).astype(y_ref.dtype)

    @pl.when(i >= nu_ref[0])
    def _():
        y_ref[...] = jnp.zeros(y_ref.shape, y_ref.dtype)


def _ffn(plan, xs, w1, w3, w2, layer):
    rows, d = xs.shape
    f = w1.shape[3]
    grid_spec = pltpu.PrefetchScalarGridSpec(
        num_scalar_prefetch=3,
        grid=(rows // TF,),
        in_specs=[pl.BlockSpec((TF, d), lambda i, tile, te, nu: (tile[i], 0)),
                  pl.BlockSpec((1, 1, d, f), lambda i, tile, te, nu: (layer, te[i], 0, 0)),
                  pl.BlockSpec((1, 1, d, f), lambda i, tile, te, nu: (layer, te[i], 0, 0)),
                  pl.BlockSpec((1, 1, f, d), lambda i, tile, te, nu: (layer, te[i], 0, 0))],
        out_specs=pl.BlockSpec((TF, d), lambda i, tile, te, nu: (i, 0)),
        scratch_shapes=[pltpu.VMEM((d, f), BF16), pltpu.VMEM((d, f), BF16), pltpu.VMEM((f, d), BF16)],
    )
    return pl.pallas_call(
        _ffn_kernel,
        grid_spec=grid_spec,
        out_shape=jax.ShapeDtypeStruct((rows, d), BF16),
        compiler_params=pltpu.CompilerParams(dimension_semantics=("arbitrary",),
                                             vmem_limit_bytes=VMEM_LIMIT),
        name="moe_ffn",
    )(plan["tile"], plan["tile_expert"], plan["n_used"], xs, w1, w3, w2)


def _combine_kernel(*refs, final, ptd):
    cnt_ref, loff_ref, goff_ref, route_ref, ys_hbm, xn_ref, mod_ref = refs[:7]
    if final:
        fg_ref, yp_ref, ysm_ref, stage, sem = refs[7:]
    else:
        xo_ref, stage, sem = refs[7:]
    i = pl.program_id(0)
    n = pl.num_programs(0)
    slot = i % 2
    copies = functools.partial(_chunk_copies, cnt_ref, loff_ref, goff_ref, to_sorted=False)

    @pl.when(i == 0)
    def _():
        stage[...] = jnp.zeros(stage.shape, stage.dtype)
        copies(i, stage.at[0], ys_hbm, sem.at[0], wait=False)

    @pl.when(i + 1 < n)
    def _():
        copies(i + 1, stage.at[1 - slot], ys_hbm, sem.at[1 - slot], wait=False)

    route = route_ref[...]
    p0, p1 = _perm(route, loff_ref, i)
    gate_rows = jnp.where(p0, route[2:3], jnp.where(p1, route[3:4], 0.0))
    gate_col = jnp.sum(gate_rows, axis=1, keepdims=True)
    p = jnp.where(p0, 1.0, jnp.where(p1, 1.0, 0.0)).astype(BF16)
    copies(i, stage.at[slot], ys_hbm, sem.at[slot], wait=True)
    scaled = (stage[slot].astype(F32) * gate_col).astype(BF16)
    y = lax.dot_general(p, scaled, (((0,), (0,)), ((), ())), preferred_element_type=F32)
    xo = xn_ref[...] + mod_ref[0, 0][5:6] * y
    if final:
        out = _rms(xo, fg_ref[...])

        @pl.when(i < ptd)
        def _():
            yp_ref[...] = out

        @pl.when(i >= ptd)
        def _():
            ysm_ref[...] = out
    else:
        xo_ref[...] = xo


def _combine(plan, route, ys, xn, mod, final_g, *, layer, final, p_tok, s_seq):
    t_tok, d = xn.shape
    nd = t_tok // TD
    ptd = p_tok // TD
    std = s_seq // TD
    in_specs = [pl.BlockSpec((8, TD), lambda i, *_: (0, i)),
                pl.BlockSpec(memory_space=pl.ANY),
                pl.BlockSpec((TD, d), lambda i, *_: (i, 0)),
                pl.BlockSpec((1, 1, N_MOD, d),
                             lambda i, *_: (layer, jnp.where(i < ptd, 0, 1 + (i - ptd) // std), 0, 0))]
    args = [route, ys, xn, mod]
    if final:
        in_specs.append(pl.BlockSpec((1, d), lambda i, *_: (0, 0)))
        args.append(final_g)
        out_specs = [pl.BlockSpec((TD, d), lambda i, *_: (jnp.minimum(i, ptd - 1), 0)),
                     pl.BlockSpec((TD, d), lambda i, *_: (jnp.maximum(i - ptd, 0), 0))]
        out_shape = [jax.ShapeDtypeStruct((p_tok, d), F32),
                     jax.ShapeDtypeStruct((t_tok - p_tok, d), F32)]
    else:
        out_specs = pl.BlockSpec((TD, d), lambda i, *_: (i, 0))
        out_shape = jax.ShapeDtypeStruct((t_tok, d), F32)
    grid_spec = pltpu.PrefetchScalarGridSpec(
        num_scalar_prefetch=3, grid=(nd,), in_specs=in_specs, out_specs=out_specs,
        scratch_shapes=[pltpu.VMEM((2, _stage_rows(), d), BF16), pltpu.SemaphoreType.DMA((2,))])
    return pl.pallas_call(
        functools.partial(_combine_kernel, final=final, ptd=ptd),
        grid_spec=grid_spec, out_shape=out_shape,
        compiler_params=pltpu.CompilerParams(dimension_semantics=("arbitrary",),
                                             vmem_limit_bytes=VMEM_LIMIT),
        name="moe_combine_final" if final else "moe_combine",
    )(plan["cnt16"], plan["loff"], plan["goff"], *args)


def kernel(x_prompt, x_sample, cache_even_k, cache_even_v, cache_odd_k, cache_odd_v, c, c_ctx, w_ada, b_ada, norm_g, final_g, even_w_in, even_w_out, even_lam, even_subln_g, even_conv_w, even_conv_b, even_conv_norm_g, even_conv_norm_b, odd_w_qkv, odd_w_out, odd_rpb, router_w, router_b, moe_w1, moe_w3, moe_w2):
    batch, seq, d = x_prompt.shape
    dbatch, dseq, _ = x_sample.shape
    past = cache_even_k.shape[2]
    depth = w_ada.shape[0]
    p_tok, s_tok = batch * seq, dbatch * dseq
    t_tok = p_tok + s_tok
    assert TM % seq == 0 and seq % CONV_ROWS == 0 and seq >= HALO
    assert dseq % TM == 0 and dseq % TD == 0 and p_tok % TM == 0 and p_tok % TD == 0 and p_tok % dseq == 0
    assert 1 + dbatch <= COND_ROWS and dseq % GRID_W == 0 and past % TD == 0
    assert dseq // GRID_W >= NA_KROWS + NA_QROWS

    cond = jnp.zeros((COND_ROWS, d), F32).at[0].set(c_ctx).at[1:1 + dbatch].set(c)
    mod = _ada(cond, w_ada, b_ada).reshape(depth, COND_ROWS, N_MOD, d)
    rope = _rope_tables(dseq)
    rwh, rwl = _split_bf16(jnp.pad(router_w, ((0, 0), (0, LANE - N_EXPERTS))))
    rw2 = jnp.concatenate([rwh, rwl], axis=1)
    rbias = router_b.reshape(N_EXPERTS, 1)
    xs = [x_prompt.reshape(p_tok, d), x_sample.reshape(s_tok, d)]
    outs = {}

    for layer in range(depth):
        j = layer // 2
        even = layer % 2 == 0
        g1 = norm_g[layer, 0].reshape(1, d)
        g2 = norm_g[layer, 1].reshape(1, d)
        if even:
            q, k, v, u, kf, vf = _pre(xs, mod, g1, even_w_in[j].astype(BF16), rope, layer=layer,
                                      even=True, p_tok=p_tok, p_seq=seq, s_seq=dseq,
                                      heads=(A_HEADS, A_V_DIM))
            outs["even_k"], outs["even_v"] = kf, vf
            lam_init = 0.8 - 0.6 * float(np.exp(-0.3 * layer))
            lam = even_lam[j]
            subln = even_subln_g[j].reshape(1, A_V_DIM)
            ap = _pair_attn(q, k, v, batch=batch, n_pairs=A_HEADS, nq_tok=seq, nk_tok=seq, q_row0=0,
                            k_row0=0, tq=seq, tk=seq, groups=PROMPT_GROUPS, diff=True, lam=lam,
                            subln=subln, lam_init=lam_init)
            ctx = (cache_even_k.reshape(dbatch, -1, past, A_WIDTH),
                   cache_even_v.reshape(dbatch, -1, past, A_WIDTH), j)
            a_s = _pair_attn(q, k, v, batch=dbatch, n_pairs=A_HEADS, nq_tok=dseq, nk_tok=dseq,
                             q_row0=p_tok, k_row0=p_tok, tq=ATT_TQ, tk=ATT_TK, groups=1, ctx=ctx, diff=True,
                             lam=lam, subln=subln, lam_init=lam_init)
            conv = [even_conv_w[j], even_conv_b[j].reshape(1, -1), even_conv_norm_g[j].reshape(1, -1),
                    even_conv_norm_b[j].reshape(1, -1)]
            xn, h2, logits = _post((ap, a_s), u, conv, even_w_out[j].astype(BF16), xs, mod, g2, rw2,
                                   layer=layer, even=True, p_tok=p_tok, p_seq=seq, s_seq=dseq)
        else:
            q, k, v, kf, vf = _pre(xs, mod, g1, odd_w_qkv[j].astype(BF16), None, layer=layer,
                                   even=False, p_tok=p_tok, p_seq=seq, s_seq=dseq,
                                   heads=(C_HEADS, C_HEAD_DIM))
            outs["odd_k"] = kf.reshape(batch, 1, seq, C_HEADS, C_HEAD_DIM)
            outs["odd_v"] = vf.reshape(batch, 1, seq, C_HEADS, C_HEAD_DIM)
            n_pairs = d // LANE
            op = _pair_attn(q, k, v, batch=batch, n_pairs=n_pairs, nq_tok=seq, nk_tok=seq, q_row0=0,
                            k_row0=0, tq=seq, tk=seq, groups=PROMPT_GROUPS)
            bias = _na_bias(odd_rpb[j])
            o_s = _na(q, k, v, cache_odd_k.reshape(dbatch, -1, past, d),
                      cache_odd_v.reshape(dbatch, -1, past, d), j, bias,
                      batch=dbatch, n_tok=dseq, row0=p_tok)
            xn, h2, logits = _post((op, o_s), None, None, odd_w_out[j].astype(BF16), xs, mod, g2, rw2,
                                   layer=layer, even=False, p_tok=p_tok, p_seq=seq, s_seq=dseq)
        route, cnt = _route(logits, rbias)
        plan = _moe_plan(cnt, t_tok)
        x_sorted = _dispatch(plan, route, h2)
        y_sorted = _ffn(plan, x_sorted, moe_w1, moe_w3, moe_w2, layer)
        final = layer == depth - 1
        res = _combine(plan, route, y_sorted, xn, mod, final_g.reshape(1, d), layer=layer, final=final,
                       p_tok=p_tok, s_seq=dseq)
        if final:
            y_prompt, y_sample = res
        else:
            xs = [res]

    return (y_prompt.reshape(batch, seq, d), y_sample.reshape(dbatch, dseq, d),
            outs["even_k"], outs["even_v"], outs["odd_k"], outs["odd_v"])
```

```python
import functools

import numpy as np
import jax
import jax.numpy as jnp
from jax import lax
from jax.experimental import pallas as pl
from jax.experimental.pallas import tpu as pltpu

F32 = jnp.float32
BF16 = jnp.bfloat16
I32 = jnp.int32

GRID_W = 64
EPS = 1e-6
N_MOD = 6
A_HEADS = 4
A_QK_DIM = 64
A_V_DIM = 128
A_WIDTH = A_HEADS * A_V_DIM
CONV_WIDTH = 31
ROPE_BASE = 10000.0
ROPE_FREQS = A_QK_DIM // 4
C_HEADS = 16
C_HEAD_DIM = 64
NA_ROWS = 8
NA_COLS = 16
N_EXPERTS = 16
N_GROUPS = 4
EXPERTS_PER_GROUP = N_EXPERTS // N_GROUPS
NEG = -1e30
LOG2E = 1.4426950408889634

LANE = 128
SUBLANE = 8
BF16_ROWS = 16
VMEM_LIMIT = 52 * 1024 * 1024

TM = 512
ATT_TQ = 512
ATT_TK = 512
TD = 512
TF = 512
HALO = 16
CONV_ROWS = 64
ROUTE_TILES = 8
NA_QROWS = 4
NA_KROWS = NA_QROWS + NA_ROWS
NA_GROUPS = 16
PROMPT_GROUPS = 4
COND_ROWS = 16


def _nt(a, b):
    return lax.dot_general(a, b, (((1,), (1,)), ((), ())), preferred_element_type=F32)


def _mm(a, b):
    return jnp.dot(a, b, preferred_element_type=F32)


def _split_bf16(a):
    hi = a.astype(BF16)
    lo = (a - hi.astype(F32)).astype(BF16)
    return hi, lo


def _sigmoid(x):
    return 1.0 / (1.0 + jnp.exp(-x))


def _rms(x, g):
    return x * lax.rsqrt(jnp.mean(x * x, axis=-1, keepdims=True) + EPS) * g


def _ada_kernel(cond_ref, w_ref, b_ref, o_ref):
    c = cond_ref[...]
    a = c * _sigmoid(c)
    ah, al = _split_bf16(a)
    wh, wl = _split_bf16(w_ref[0])
    o_ref[0] = _mm(ah, wh) + _mm(ah, wl) + _mm(al, wh) + b_ref[0]


def _ada(cond, w_ada, b_ada):
    depth, d, n = w_ada.shape
    tn = 1536
    return pl.pallas_call(
        _ada_kernel,
        grid=(depth, n // tn),
        in_specs=[pl.BlockSpec((COND_ROWS, d), lambda l, j: (0, 0)),
                  pl.BlockSpec((1, d, tn), lambda l, j: (l, 0, j)),
                  pl.BlockSpec((1, 1, tn), lambda l, j: (l, 0, j))],
        out_specs=pl.BlockSpec((1, COND_ROWS, tn), lambda l, j: (l, 0, j)),
        out_shape=jax.ShapeDtypeStruct((depth, COND_ROWS, n), F32),
        compiler_params=pltpu.CompilerParams(dimension_semantics=("arbitrary", "arbitrary"),
                                             vmem_limit_bytes=VMEM_LIMIT),
        name="ada",
    )(cond, w_ada, b_ada.reshape(depth, 1, n))


def _load_x(x_refs, t, pt):
    if len(x_refs) == 1:
        return x_refs[0][...]
    return jnp.where(t < pt, x_refs[0][...], x_refs[1][...])


def _store_heads(ref, val):
    if len(ref.shape) == 2:
        ref[...] = val
        return
    nb, _, seq, heads, hd = ref.shape
    for b in range(nb):
        for h in range(heads):
            ref[b, 0, :, h, :] = val[b * seq:(b + 1) * seq, h * hd:(h + 1) * hd]


def _pre_even_kernel(*refs, n_x, pt):
    x_refs = refs[:n_x]
    (mod_ref, g_ref, w_ref, c_ref, s1_ref, s2_ref,
     q_ref, k_ref, v_ref, u_ref, kf_ref, vf_ref) = refs[n_x:]
    t = pl.program_id(0)
    x = _load_x(x_refs, t, pt)
    mod = mod_ref[0, 0]
    h = _rms(x, g_ref[...]) * (1.0 + mod[1:2]) + mod[0:1]
    proj = _mm(h.astype(BF16), w_ref[...])
    aw = A_WIDTH
    q = proj[:, :aw]
    k = proj[:, aw:2 * aw]
    v = proj[:, 2 * aw:3 * aw]
    bw = (proj.shape[1] - 3 * aw) // 2
    gv = proj[:, 3 * aw:3 * aw + bw]
    gg = proj[:, 3 * aw + bw:]
    v_ref[...] = v.astype(BF16)
    u_ref[...] = (gv * _sigmoid(gg)).astype(BF16)
    scale = A_QK_DIM ** -0.5 * LOG2E

    @pl.when(t < pt)
    def _():
        q_ref[...] = (q * scale).astype(BF16)
        k_ref[...] = k.astype(BF16)
        _store_heads(kf_ref, k)
        _store_heads(vf_ref, v)

    @pl.when(t >= pt)
    def _():
        reps = aw // LANE
        c = jnp.concatenate([c_ref[...]] * reps, axis=1)
        s1 = jnp.concatenate([s1_ref[...]] * reps, axis=1)
        s2 = jnp.concatenate([s2_ref[...]] * reps, axis=1)

        def rope(z):
            return (z * c + pltpu.roll(z, ROPE_FREQS, 1) * s1
                    + pltpu.roll(z, aw - ROPE_FREQS, 1) * s2)

        q_ref[...] = (rope(q) * scale).astype(BF16)
        k_ref[...] = rope(k).astype(BF16)


def _pre_odd_kernel(*refs, n_x, pt):
    x_refs = refs[:n_x]
    mod_ref, g_ref, w_ref, q_ref, k_ref, v_ref, kf_ref, vf_ref = refs[n_x:]
    t = pl.program_id(0)
    x = _load_x(x_refs, t, pt)
    mod = mod_ref[0, 0]
    h = _rms(x, g_ref[...]) * (1.0 + mod[1:2]) + mod[0:1]
    proj = _mm(h.astype(BF16), w_ref[...])
    d = proj.shape[1] // 3
    q = proj[:, :d]
    k = proj[:, d:2 * d]
    v = proj[:, 2 * d:]
    q_ref[...] = (q * (C_HEAD_DIM ** -0.5 * LOG2E)).astype(BF16)
    k_ref[...] = k.astype(BF16)
    v_ref[...] = v.astype(BF16)

    @pl.when(t < pt)
    def _():
        _store_heads(kf_ref, k)
        _store_heads(vf_ref, v)


def _x_specs(xs, pt, d):
    if len(xs) == 1:
        return [pl.BlockSpec((TM, d), lambda t: (t, 0))]
    return [pl.BlockSpec((TM, d), lambda t: (jnp.minimum(t, pt - 1), 0)),
            pl.BlockSpec((TM, d), lambda t: (jnp.maximum(t - pt, 0), 0))]


def _pre(xs, mod, g, w, rope, *, layer, even, p_tok, p_seq, s_seq, heads):
    d = xs[0].shape[1]
    nb = TM // p_seq
    if heads[1] % LANE == 0:
        cache_spec = pl.BlockSpec((nb, 1, p_seq) + heads, lambda t: (jnp.minimum(t, pt - 1), 0, 0, 0, 0))
        cache_shape = jax.ShapeDtypeStruct((p_tok // p_seq, 1, p_seq) + heads, F32)
    else:
        cache_spec = pl.BlockSpec((TM, d), lambda t: (jnp.minimum(t, pt - 1), 0))
        cache_shape = jax.ShapeDtypeStruct((p_tok, d), F32)
    t_tok = sum(x.shape[0] for x in xs) if len(xs) == 2 else xs[0].shape[0]
    nt = t_tok // TM
    pt = p_tok // TM
    st = s_seq // TM
    x_specs = _x_specs(xs, pt, d)
    mod_spec = pl.BlockSpec((1, 1, N_MOD, d),
                            lambda t: (layer, jnp.where(t < pt, 0, 1 + (t - pt) // st), 0, 0))
    g_spec = pl.BlockSpec((1, d), lambda t: (0, 0))
    w_spec = pl.BlockSpec(w.shape, lambda t: (0, 0))
    tok = lambda width: pl.BlockSpec((TM, width), lambda t: (t, 0))
    params = pltpu.CompilerParams(dimension_semantics=("arbitrary",), vmem_limit_bytes=VMEM_LIMIT)
    if even:
        aw = A_WIDTH
        bw = (w.shape[1] - 3 * aw) // 2
        rspec = pl.BlockSpec((TM, LANE), lambda t: (jnp.maximum(t - pt, 0) % st, 0))
        return pl.pallas_call(
            functools.partial(_pre_even_kernel, n_x=len(xs), pt=pt),
            grid=(nt,),
            in_specs=x_specs + [mod_spec, g_spec, w_spec, rspec, rspec, rspec],
            out_specs=[tok(aw), tok(aw), tok(aw), tok(bw), cache_spec, cache_spec],
            out_shape=[jax.ShapeDtypeStruct((t_tok, aw), BF16)] * 3
            + [jax.ShapeDtypeStruct((t_tok, bw), BF16)] + [cache_shape] * 2,
            compiler_params=params, name="pre_even",
        )(*xs, mod, g, w, *rope)
    return pl.pallas_call(
        functools.partial(_pre_odd_kernel, n_x=len(xs), pt=pt),
        grid=(nt,),
        in_specs=x_specs + [mod_spec, g_spec, w_spec],
        out_specs=[tok(d), tok(d), tok(d), cache_spec, cache_spec],
        out_shape=[jax.ShapeDtypeStruct((t_tok, d), BF16)] * 3 + [cache_shape] * 2,
        compiler_params=params, name="pre_odd",
    )(*xs, mod, g, w)


def _rope_tables(n_tokens):
    t = jnp.arange(n_tokens, dtype=I32)
    pos = jnp.stack([t // GRID_W, t % GRID_W], axis=-1).astype(F32)
    freqs = ROPE_BASE ** (-jnp.arange(ROPE_FREQS, dtype=F32) / ROPE_FREQS)
    ang = pos[:, :, None] * freqs
    cos, sin = jnp.cos(ang), jnp.sin(ang)
    zero = jnp.zeros_like(sin)
    c = jnp.concatenate([cos, cos], axis=-1).reshape(n_tokens, A_QK_DIM)
    s1 = jnp.concatenate([zero, sin], axis=-1).reshape(n_tokens, A_QK_DIM)
    s2 = jnp.concatenate([-sin, zero], axis=-1).reshape(n_tokens, A_QK_DIM)
    rep = LANE // A_QK_DIM
    return tuple(jnp.tile(a, (1, rep)) for a in (c, s1, s2))


def _stack_maps(q):
    lane = lax.broadcasted_iota(I32, q.shape, 1)
    zero = jnp.zeros_like(q)
    half = LANE // 2
    return jnp.concatenate([jnp.where(lane < half, q, zero), jnp.where(lane >= half, q, zero)], axis=0)


def _pair_attn_kernel(*refs, diff, ctx, groups, tq, nk, tk, lam_init):
    refs = list(refs)
    q_ref = refs.pop(0)
    kc_ref, vc_ref = (refs.pop(0), refs.pop(0)) if ctx else (None, None)
    k_ref, v_ref = refs.pop(0), refs.pop(0)
    lam_ref, sg_ref = (refs.pop(0), refs.pop(0)) if diff else (None, None)
    o_ref = refs.pop(0)
    half = LANE // 2
    for g in range(groups):
        qs = _stack_maps(q_ref[g * tq:(g + 1) * tq, :])
        chunks = []
        if ctx:
            for j in range(kc_ref.shape[2] // tk):
                rows = slice(j * tk, (j + 1) * tk)
                chunks.append((lambda rows=rows: kc_ref[0, 0, rows, :].astype(BF16),
                               lambda rows=rows: vc_ref[0, 0, rows, :].astype(BF16)))
        for j in range(nk):
            rows = slice((g * nk + j) * tk, (g * nk + j + 1) * tk)
            chunks.append((lambda rows=rows: k_ref[rows, :], lambda rows=rows: v_ref[rows, :]))
        ones = jnp.ones((tk, LANE), BF16)
        m = acc = None
        for j, (get_k, get_v) in enumerate(chunks):
            s = _nt(qs, get_k())
            mc = jnp.max(s, axis=-1, keepdims=True)
            vx = jnp.concatenate([get_v(), ones], axis=1)
            if j == 0:
                m = mc
                acc = _mm(jnp.exp2(s - m).astype(BF16), vx)
            else:
                m_new = jnp.maximum(m, mc)
                alpha = jnp.exp2(m - m_new)
                acc = alpha * acc + _mm(jnp.exp2(s - m_new).astype(BF16), vx)
                m = m_new
        o = acc[:, :LANE] / acc[:, LANE:]
        o1 = o[0:tq]
        o2 = o[tq:]
        if diff:
            lf = lam_ref[...]
            lam = (jnp.exp(jnp.sum(lf[0:1] * lf[1:2], axis=1, keepdims=True))
                   - jnp.exp(jnp.sum(lf[2:3] * lf[3:4], axis=1, keepdims=True)) + lam_init)
            dlt = o1 - lam * o2
            out = _rms(dlt, sg_ref[...]) * (1.0 - lam_init)
        else:
            lane = lax.broadcasted_iota(I32, o1.shape, 1)
            out = jnp.where(lane < half, o1, o2)
        o_ref[g * tq:(g + 1) * tq, :] = out.astype(o_ref.dtype)


def _pair_attn(q, k, v, *, batch, n_pairs, nq_tok, nk_tok, q_row0, k_row0, tq, tk, groups,
               ctx=None, diff=False, lam=None, subln=None, lam_init=0.0):
    nk = nk_tok // tk
    if groups > 1:
        assert nq_tok == tq and batch % groups == 0 and ctx is None
        nb, nq = batch // groups, 1
    else:
        nb, nq = batch, nq_tok // tq
    qblk, kblk = groups * tq, groups * nk_tok
    qb0, kb0 = q_row0 // qblk, k_row0 // kblk
    in_specs = [pl.BlockSpec((qblk, LANE), lambda b, h, qi: (qb0 + b * nq + qi, h))]
    args = [q]
    if ctx is not None:
        ck, cv, cj = ctx
        past = ck.shape[2]
        assert past % tk == 0
        in_specs += [pl.BlockSpec((1, 1, past, LANE), lambda b, h, qi: (b, cj, 0, h))] * 2
        args += [ck, cv]
    in_specs += [pl.BlockSpec((kblk, LANE), lambda b, h, qi: (kb0 + b, h))] * 2
    args += [k, v]
    if diff:
        in_specs += [pl.BlockSpec(lam.shape, lambda b, h, qi: (0, 0)),
                     pl.BlockSpec(subln.shape, lambda b, h, qi: (0, 0))]
        args += [lam, subln]
    return pl.pallas_call(
        functools.partial(_pair_attn_kernel, diff=diff, ctx=ctx is not None, groups=groups, tq=tq,
                          nk=nk, tk=tk, lam_init=lam_init),
        grid=(nb, n_pairs, nq),
        in_specs=in_specs,
        out_specs=pl.BlockSpec((qblk, LANE), lambda b, h, qi: (b * nq + qi, h)),
        out_shape=jax.ShapeDtypeStruct((batch * nq_tok, n_pairs * LANE), BF16),
        compiler_params=pltpu.CompilerParams(
            dimension_semantics=("arbitrary",) * 3, vmem_limit_bytes=VMEM_LIMIT),
        name="diff_attn" if diff else "pair_attn",
    )(*args)


def _na_kernel(q_ref, k_ref, v_ref, kc_ref, vc_ref, colb_ref, o_ref, bias_ref, *, nrb, n_tok, groups):
    tq = NA_QROWS * GRID_W
    tkw = NA_KROWS * GRID_W
    half = LANE // 2
    rows = n_tok // GRID_W

    @pl.when(jnp.logical_and(pl.program_id(1) == 0, pl.program_id(2) == 0))
    def _():
        lane = lax.broadcasted_iota(I32, (GRID_W, LANE), 1)
        masked = jnp.full((GRID_W, LANE), NEG * LOG2E, F32)
        for x, rb in enumerate((0, 1, nrb - 1)):
            start = min(max(NA_QROWS * rb - NA_QROWS, 0), rows - NA_KROWS)
            for hh in range(2):
                for i in range(NA_QROWS):
                    r = NA_QROWS * rb + i
                    rs = min(max(r - NA_ROWS // 2, 0), rows - NA_ROWS)
                    row0 = (hh * NA_QROWS + i) * GRID_W
                    for jp in range(NA_KROWS // 2):
                        kr = start + 2 * jp
                        ok_l = rs <= kr < rs + NA_ROWS
                        ok_r = rs <= kr + 1 < rs + NA_ROWS
                        if ok_l or ok_r:
                            blk = colb_ref[hh, kr - r + NA_ROWS]
                            if not ok_l:
                                blk = jnp.where(lane >= half, blk, masked)
                            if not ok_r:
                                blk = jnp.where(lane < half, blk, masked)
                        else:
                            blk = masked
                        bias_ref[x, row0:row0 + GRID_W, jp * LANE:(jp + 1) * LANE] = blk

    kc = kc_ref[0, 0].astype(BF16)
    vc = jnp.concatenate([vc_ref[0, 0].astype(BF16), jnp.ones((kc.shape[0], LANE), BF16)], axis=1)
    ones = jnp.ones((tkw, LANE), BF16)
    for g in range(groups):
        rb = pl.program_id(2) * groups + g
        qs = _stack_maps(q_ref[g * tq:(g + 1) * tq, :])
        start = pl.multiple_of(jnp.clip(rb * tq - tq, 0, n_tok - tkw), tq)
        case = jnp.where(rb == 0, 0, jnp.where(rb == nrb - 1, 2, 1))
        s_loc = _nt(qs, k_ref[pl.ds(start, tkw), :]) + bias_ref[case]
        s_ctx = _nt(qs, kc)
        m = jnp.maximum(jnp.max(s_loc, axis=-1, keepdims=True), jnp.max(s_ctx, axis=-1, keepdims=True))
        v_loc = jnp.concatenate([v_ref[pl.ds(start, tkw), :], ones], axis=1)
        acc = _mm(jnp.exp2(s_loc - m).astype(BF16), v_loc) + _mm(jnp.exp2(s_ctx - m).astype(BF16), vc)
        o = acc[:, :LANE] / acc[:, LANE:]
        lane = lax.broadcasted_iota(I32, (tq, LANE), 1)
        o_ref[g * tq:(g + 1) * tq, :] = jnp.where(lane < half, o[0:tq], o[tq:]).astype(o_ref.dtype)


def _na_bias(rpb):
    h = rpb.shape[0]
    nd = 2 * NA_COLS - 1
    c = np.arange(GRID_W)[:, None]
    kc = np.arange(GRID_W)[None, :]
    cs = np.clip(c - NA_COLS // 2, 0, GRID_W - NA_COLS)
    okc = (kc >= cs) & (kc < cs + NA_COLS)
    dc = np.where(okc, kc - c + NA_COLS - 1, -1)
    onehot = (dc.reshape(1, -1) == np.arange(nd)[:, None]).astype(np.float32)
    colmask = np.where(okc, 0.0, NEG).astype(np.float32).reshape(1, 1, GRID_W, GRID_W)
    colb = jnp.dot(rpb.astype(F32).reshape(-1, nd), onehot, precision=lax.Precision.HIGHEST)
    colb = (colb.reshape(h, 2 * NA_ROWS - 1, GRID_W, GRID_W) + colmask) * LOG2E
    masked = jnp.full((h, 1, GRID_W, GRID_W), NEG * LOG2E, F32)
    colb = jnp.concatenate([masked, colb, masked], axis=1)
    return jnp.concatenate([colb[:, :-1], colb[:, 1:]], axis=-1)


def _na(q, k, v, kc, vc, cj, colb, *, batch, n_tok, row0):
    d = q.shape[1]
    n_pairs = d // LANE
    tq = NA_QROWS * GRID_W
    nrb = n_tok // tq
    past = kc.shape[2]
    groups = NA_GROUPS
    while nrb % groups:
        groups //= 2
    qblk = groups * tq
    ns = nrb // groups
    qb0 = row0 // qblk
    kb0 = row0 // n_tok
    return pl.pallas_call(
        functools.partial(_na_kernel, nrb=nrb, n_tok=n_tok, groups=groups),
        grid=(n_pairs, batch, ns),
        in_specs=[pl.BlockSpec((qblk, LANE), lambda h, b, r: (qb0 + b * ns + r, h)),
                  pl.BlockSpec((n_tok, LANE), lambda h, b, r: (kb0 + b, h)),
                  pl.BlockSpec((n_tok, LANE), lambda h, b, r: (kb0 + b, h)),
                  pl.BlockSpec((1, 1, past, LANE), lambda h, b, r: (b, cj, 0, h)),
                  pl.BlockSpec((1, 1, past, LANE), lambda h, b, r: (b, cj, 0, h)),
                  pl.BlockSpec((2,) + colb.shape[1:], lambda h, b, r: (h, 0, 0, 0))],
        out_specs=pl.BlockSpec((qblk, LANE), lambda h, b, r: (b * ns + r, h)),
        out_shape=jax.ShapeDtypeStruct((batch * n_tok, d), BF16),
        scratch_shapes=[pltpu.VMEM((3, 2 * tq, NA_KROWS * GRID_W), F32)],
        compiler_params=pltpu.CompilerParams(
            dimension_semantics=("arbitrary",) * 3, vmem_limit_bytes=VMEM_LIMIT),
        name="na_attn",
    )(q, k, v, kc, vc, colb)


def _route_kernel(logit_ref, rb_ref, route_ref, cnt_ref):
    parts = [logit_ref[j] for j in range(logit_ref.shape[0])]
    logits = parts[0] if len(parts) == 1 else jnp.concatenate(parts, axis=1)
    scores = _sigmoid(logits)
    sel = scores + rb_ref[...]
    tm = sel.shape[1]
    gs = []
    for g in range(N_GROUPS):
        r = [sel[g * EXPERTS_PER_GROUP + a:g * EXPERTS_PER_GROUP + a + 1] for a in range(EXPERTS_PER_GROUP)]
        best = None
        for a in range(EXPERTS_PER_GROUP):
            for b in range(a + 1, EXPERTS_PER_GROUP):
                pair = r[a] + r[b]
                best = pair if best is None else jnp.maximum(best, pair)
        gs.append(best)
    bg = jnp.zeros((1, tm), I32)
    bs = gs[0]
    for g in range(1, N_GROUPS):
        better = gs[g] > bs
        bg = jnp.where(better, g, bg)
        bs = jnp.where(better, gs[g], bs)
    eidx = lax.broadcasted_iota(I32, sel.shape, 0)
    masked = jnp.where(eidx // EXPERTS_PER_GROUP == bg, sel, -jnp.inf)
    m1 = jnp.max(masked, axis=0, keepdims=True)
    i1 = jnp.min(jnp.where(masked == m1, eidx, N_EXPERTS), axis=0, keepdims=True)
    masked2 = jnp.where(eidx == i1, -jnp.inf, masked)
    m2 = jnp.max(masked2, axis=0, keepdims=True)
    i2 = jnp.min(jnp.where(masked2 == m2, eidx, N_EXPERTS), axis=0, keepdims=True)
    s1 = jnp.sum(jnp.where(eidx == i1, scores, 0.0), axis=0, keepdims=True)
    s2 = jnp.sum(jnp.where(eidx == i2, scores, 0.0), axis=0, keepdims=True)
    tot = s1 + s2
    route_ref[...] = jnp.concatenate(
        [i1.astype(F32), i2.astype(F32), s1 / tot, s2 / tot, jnp.zeros((4, tm), F32)], axis=0)
    chosen = jnp.where(eidx == i1, 1.0, jnp.where(eidx == i2, 1.0, 0.0))
    lane = lax.broadcasted_iota(I32, (N_EXPERTS, LANE), 1)
    cnt = jnp.zeros((N_EXPERTS, LANE), F32)
    for j in range(tm // TD):
        cj = jnp.sum(chosen[:, j * TD:(j + 1) * TD], axis=1, keepdims=True)
        cnt = jnp.where(lane == j, cj, cnt)
    cnt_ref[0] = cnt


def _route(logits, rbias):
    nt = logits.shape[0]
    t_tok = nt * TM
    step = ROUTE_TILES
    while nt % step:
        step //= 2
    span = step * TM
    assert span % TD == 0
    route, cnt = pl.pallas_call(
        _route_kernel,
        grid=(nt // step,),
        in_specs=[pl.BlockSpec((step, N_EXPERTS, TM), lambda i: (i, 0, 0)),
                  pl.BlockSpec(rbias.shape, lambda i: (0, 0))],
        out_specs=[pl.BlockSpec((8, span), lambda i: (0, i)),
                   pl.BlockSpec((1, N_EXPERTS, LANE), lambda i: (i, 0, 0))],
        out_shape=[jax.ShapeDtypeStruct((8, t_tok), F32),
                   jax.ShapeDtypeStruct((nt // step, N_EXPERTS, LANE), F32)],
        compiler_params=pltpu.CompilerParams(dimension_semantics=("arbitrary",),
                                             vmem_limit_bytes=VMEM_LIMIT),
        name="route",
    )(logits, rbias)
    per = span // TD
    cnt = cnt[:, :, :per].astype(I32).transpose(0, 2, 1).reshape(t_tok // TD, N_EXPERTS)
    return route, cnt


def _post_kernel(*refs, even, n_x, pt, st):
    a_refs, refs = refs[:2], refs[2:]
    if even:
        (up_ref, uc_ref, un_ref, cw_ref, cb_ref, cg_ref, cnb_ref, wo_ref) = refs[:8]
        rest = refs[8:]
    else:
        wo_ref = refs[0]
        rest = refs[1:]
    x_refs = rest[:n_x]
    (mod_ref, g2_ref, rw2_ref, xn_ref, h2_ref, logit_ref) = rest[n_x:n_x + 6]
    t = pl.program_id(0)
    a = _load_x(a_refs, t, pt)
    if even:
        ext, shifted = rest[n_x + 6:]
        n_sub = ext.shape[0]
        sub = TM // n_sub
        latent = t >= pt
        js = jnp.maximum(t - pt, 0) % st
        pad = CONV_WIDTH // 2
        span = shifted.shape[2]
        blocks = []
        for s in range(n_sub):
            if s == 0:
                prev, has_prev = up_ref[...], jnp.logical_and(latent, js != 0)
            else:
                prev, has_prev = uc_ref[s * sub - HALO:s * sub, :], latent
            if s == n_sub - 1:
                nxt, has_next = un_ref[...], jnp.logical_and(latent, js != st - 1)
            else:
                nxt, has_next = uc_ref[(s + 1) * sub:(s + 1) * sub + HALO, :], latent
            prev = prev.astype(F32)
            nxt = nxt.astype(F32)
            ext[s, 0:HALO, :] = jnp.where(has_prev, prev, jnp.zeros_like(prev))
            ext[s, HALO:HALO + sub, :] = uc_ref[s * sub:(s + 1) * sub, :].astype(F32)
            ext[s, HALO + sub:, :] = jnp.where(has_next, nxt, jnp.zeros_like(nxt))
            for ph in range(1, SUBLANE):
                shifted[s, ph - 1] = ext[s, pl.ds(ph, span), :]
            for r0 in range(0, sub, CONV_ROWS):
                acc = jnp.zeros((CONV_ROWS, ext.shape[2]), F32) + cb_ref[...]
                for j in range(CONV_WIDTH):
                    whole, ph = divmod(HALO - pad + j, SUBLANE)
                    rows = slice(r0 + whole * SUBLANE, r0 + whole * SUBLANE + CONV_ROWS)
                    tap = ext[s, rows, :] if ph == 0 else shifted[s, ph - 1, rows, :]
                    acc = acc + cw_ref[j:j + 1, :] * tap
                blocks.append(acc)
        acc = jnp.concatenate(blocks, axis=0)
        mu = jnp.mean(acc, axis=-1, keepdims=True)
        var = jnp.mean(jnp.square(acc - mu), axis=-1, keepdims=True)
        y = (acc - mu) * lax.rsqrt(var + EPS) * cg_ref[...] + cnb_ref[...]
        ua = (y * _sigmoid(y)).astype(BF16)
        aw = a.shape[1]
        mix = _mm(a, wo_ref[0:aw, :]) + _mm(ua, wo_ref[aw:, :])
    else:
        mix = _mm(a, wo_ref[...])
    x = _load_x(x_refs, t, pt)
    mod = mod_ref[0, 0]
    xn = x + mod[2:3] * mix
    xn_ref[...] = xn
    h2 = _rms(xn, g2_ref[...]) * (1.0 + mod[4:5]) + mod[3:4]
    h2_ref[...] = h2.astype(BF16)
    hh, hl = _split_bf16(h2)
    both = _mm(hh, rw2_ref[...])
    lt = both[:, :LANE] + both[:, LANE:] + _mm(hl, rw2_ref[:, 0:LANE])
    logit_ref[0] = lt.T[0:N_EXPERTS, :]


def _post(a_parts, u, conv, wo, xs, mod, g2, rw2, *, layer, even, p_tok, p_seq, s_seq):
    d = wo.shape[1]
    t_tok = a_parts[0].shape[0] + a_parts[1].shape[0]
    nt = t_tok // TM
    pt = p_tok // TM
    st = s_seq // TM
    tok = lambda width: pl.BlockSpec((TM, width), lambda t: (t, 0))
    full = lambda arr: pl.BlockSpec(arr.shape, lambda t: (0,) * arr.ndim)
    in_specs = _x_specs(a_parts, pt, a_parts[0].shape[1])
    args = list(a_parts)
    scratch = []
    if even:
        bw = u.shape[1]
        hb = TM // HALO
        in_specs += [pl.BlockSpec((HALO, bw), lambda t: (jnp.maximum(t * hb - 1, 0), 0)),
                     tok(bw),
                     pl.BlockSpec((HALO, bw), lambda t: (jnp.minimum((t + 1) * hb, nt * hb - 1), 0))]
        args += [u, u, u]
        for arr in conv:
            in_specs.append(full(arr))
            args.append(arr)
        in_specs.append(full(wo))
        args.append(wo)
        reach = (HALO + CONV_WIDTH // 2) // SUBLANE * SUBLANE
        n_sub = TM // p_seq
        scratch = [pltpu.VMEM((n_sub, p_seq + 2 * HALO, bw), F32),
                   pltpu.VMEM((n_sub, SUBLANE - 1, p_seq + reach, bw), F32)]
    else:
        in_specs.append(full(wo))
        args.append(wo)
    in_specs += _x_specs(xs, pt, d)
    args += list(xs)
    in_specs += [pl.BlockSpec((1, 1, N_MOD, d),
                              lambda t: (layer, jnp.where(t < pt, 0, 1 + (t - pt) // st), 0, 0)),
                 full(g2), full(rw2)]
    args += [mod, g2, rw2]
    return pl.pallas_call(
        functools.partial(_post_kernel, even=even, n_x=len(xs), pt=pt, st=st),
        grid=(nt,),
        in_specs=in_specs,
        out_specs=[tok(d), tok(d), pl.BlockSpec((1, N_EXPERTS, TM), lambda t: (t, 0, 0))],
        out_shape=[jax.ShapeDtypeStruct((t_tok, d), F32), jax.ShapeDtypeStruct((t_tok, d), BF16),
                   jax.ShapeDtypeStruct((nt, N_EXPERTS, TM), F32)],
        scratch_shapes=scratch,
        compiler_params=pltpu.CompilerParams(dimension_semantics=("arbitrary",),
                                             vmem_limit_bytes=VMEM_LIMIT),
        name="post_even" if even else "post_odd",
    )(*args)


def _stage_rows():
    return 2 * TD + N_EXPERTS * BF16_ROWS


def _chunk_bits():
    bits = []
    b = TD
    while b >= BF16_ROWS:
        bits.append(b)
        b //= 2
    return bits


def _moe_plan(cnt, t_tok):
    cnt16 =(cnt + BF16_ROWS - 1) // BF16_ROWS * BF16_ROWS
    loff = jnp.cumsum(cnt16, axis=1) - cnt16
    tot = cnt16.sum(axis=0)
    reg = (tot + TF - 1) // TF * TF
    off = jnp.cumsum(reg) - reg
    goff = off[None, :] + jnp.cumsum(cnt16, axis=0) - cnt16
    ends = jnp.cumsum(reg // TF)
    n_used = ends[-1]
    n_tiles = _sorted_rows(t_tok) // TF
    tile = jnp.minimum(jnp.arange(n_tiles, dtype=I32), n_used - 1)
    tile_expert = jnp.minimum(jnp.sum(tile[:, None] >= ends[None, :], axis=1), N_EXPERTS - 1).astype(I32)
    tail = jnp.stack([n_used * TF, n_tiles - n_used])
    return dict(cnt16=cnt16.reshape(-1), loff=loff.reshape(-1), goff=goff.reshape(-1),
                gap_start=jnp.concatenate([off + tot, tail[0:1]]).astype(I32),
                gap_len=jnp.concatenate([reg - tot, tail[1:2]]).astype(I32),
                tile=tile.astype(I32), tile_expert=tile_expert,
                n_used=jnp.reshape(n_used, (1,)).astype(I32))


def _sorted_rows(t_tok):
    nd = t_tok // TD
    worst = 2 * t_tok + nd * N_EXPERTS * (BF16_ROWS - 1) + N_EXPERTS * (TF - 1)
    return (worst + TF - 1) // TF * TF


def _perm(route, loff_ref, tile):
    e0 = route[0:1].astype(I32)
    e1 = route[1:2].astype(I32)
    eidx = lax.broadcasted_iota(I32, (N_EXPERTS, TD), 0)
    m0 = eidx == e0
    m1 = eidx == e1
    chosen = jnp.where(m0, 1.0, jnp.where(m1, 1.0, 0.0)).astype(BF16)
    before = (lax.broadcasted_iota(I32, (TD, TD), 0) < lax.broadcasted_iota(I32, (TD, TD), 1))
    rank = _mm(chosen, jnp.where(before, 1.0, 0.0).astype(BF16))
    ecol = lax.broadcasted_iota(I32, (N_EXPERTS, 1), 0)
    lcol = jnp.zeros((N_EXPERTS, 1), F32)
    for e in range(N_EXPERTS):
        lcol = jnp.where(ecol == e, loff_ref[tile * N_EXPERTS + e].astype(F32), lcol)
    base = rank + lcol
    slot0 = jnp.sum(jnp.where(m0, base, 0.0), axis=0, keepdims=True).astype(I32)
    slot1 = jnp.sum(jnp.where(m1, base, 0.0), axis=0, keepdims=True).astype(I32)
    rio = lax.broadcasted_iota(I32, (_stage_rows(), TD), 0)
    return rio == slot0, rio == slot1


def _chunk_copies(cnt_ref, loff_ref, goff_ref, tile, stage, sorted_hbm, sem, *, to_sorted, wait):
    if wait:
        total = cnt_ref[tile * N_EXPERTS]
        for e in range(1, N_EXPERTS):
            total = total + cnt_ref[tile * N_EXPERTS + e]
        bit = 2 * TD
        while bit >= BF16_ROWS:
            s_view = stage.at[pl.ds(0, bit), :]
            h_view = sorted_hbm.at[pl.ds(0, bit), :]
            cp = (pltpu.make_async_copy(s_view, h_view, sem) if to_sorted
                  else pltpu.make_async_copy(h_view, s_view, sem))

            @pl.when((total & bit) != 0)
            def _(cp=cp):
                cp.wait()

            bit //= 2
        return
    for e in range(N_EXPERTS):
        n = cnt_ref[tile * N_EXPERTS + e]
        lo = loff_ref[tile * N_EXPERTS + e]
        go = goff_ref[tile * N_EXPERTS + e]
        for bit in _chunk_bits():
            done = n & ~(2 * bit - 1)
            s_view = stage.at[pl.ds(pl.multiple_of(lo + done, BF16_ROWS), bit), :]
            h_view = sorted_hbm.at[pl.ds(pl.multiple_of(go + done, BF16_ROWS), bit), :]
            cp = (pltpu.make_async_copy(s_view, h_view, sem) if to_sorted
                  else pltpu.make_async_copy(h_view, s_view, sem))

            @pl.when((n & bit) != 0)
            def _(cp=cp):
                cp.wait() if wait else cp.start()


def _dispatch_kernel(cnt_ref, loff_ref, goff_ref, gs_ref, gl_ref,
                     route_ref, h2_ref, xs_hbm, stage, zeros, sem, zsem):
    i = pl.program_id(0)
    n = pl.num_programs(0)
    slot = i % 2
    p0, p1 = _perm(route_ref[...], loff_ref, i)
    p = jnp.where(p0, 1.0, jnp.where(p1, 1.0, 0.0)).astype(BF16)
    stage[slot] =_mm(p, h2_ref[...]).astype(BF16)
    copies = functools.partial(_chunk_copies, cnt_ref, loff_ref, goff_ref, to_sorted=True)
    copies(i, stage.at[slot], xs_hbm, sem.at[slot], wait=False)

    @pl.when(i > 0)
    def _():
        copies(i - 1, stage.at[1 - slot], xs_hbm, sem.at[1 - slot], wait=True)

    @pl.when(i == n - 1)
    def _():
        zeros[...] = jnp.zeros(zeros.shape, zeros.dtype)
        bits = [b for b in _chunk_bits() if b < TF]
        for wait in (False, True):
            for e in range(N_EXPERTS):
                g0, gl = gs_ref[e], gl_ref[e]
                for bit in bits:
                    done = gl & ~(2 * bit - 1)
                    cp = pltpu.make_async_copy(
                        zeros.at[pl.ds(0, bit), :],
                        xs_hbm.at[pl.ds(pl.multiple_of(g0 + done, BF16_ROWS), bit), :], zsem)

                    @pl.when((gl & bit) != 0)
                    def _(cp=cp, wait=wait):
                        cp.wait() if wait else cp.start()
        tail0, ntail = gs_ref[N_EXPERTS], gl_ref[N_EXPERTS]

        def tail_copy(j):
            return pltpu.make_async_copy(
                zeros, xs_hbm.at[pl.ds(pl.multiple_of(tail0 + j * TF, TF), TF), :], zsem)

        @pl.loop(0, ntail)
        def _(j):
            tail_copy(j).start()

        @pl.loop(0, ntail)
        def _(j):
            tail_copy(j).wait()

        copies(i, stage.at[slot], xs_hbm, sem.at[slot], wait=True)


def _dispatch(plan, route, h2):
    t_tok, d = h2.shape
    nd = t_tok // TD
    rows = _sorted_rows(t_tok)
    grid_spec = pltpu.PrefetchScalarGridSpec(
        num_scalar_prefetch=5,
        grid=(nd,),
        in_specs=[pl.BlockSpec((8, TD), lambda i, *_: (0, i)),
                  pl.BlockSpec((TD, d), lambda i, *_: (i, 0))],
        out_specs=pl.BlockSpec(memory_space=pl.ANY),
        scratch_shapes=[pltpu.VMEM((2, _stage_rows(), d), BF16), pltpu.VMEM((TF, d), BF16),
                        pltpu.SemaphoreType.DMA((2,)), pltpu.SemaphoreType.DMA(())],
    )
    return pl.pallas_call(
        _dispatch_kernel,
        grid_spec=grid_spec,
        out_shape=jax.ShapeDtypeStruct((rows, d), BF16),
        compiler_params=pltpu.CompilerParams(dimension_semantics=("arbitrary",),
                                             vmem_limit_bytes=VMEM_LIMIT),
        name="moe_dispatch",
    )(plan["cnt16"], plan["loff"], plan["goff"], plan["gap_start"], plan["gap_len"], route, h2)


def _ffn_kernel(tile_ref, te_ref, nu_ref, x_ref, w1_ref, w3_ref, w2_ref, y_ref, w1b, w3b, w2b):
    i = pl.program_id(0)

    @pl.when(jnp.logical_or(i == 0, te_ref[i] != te_ref[jnp.maximum(i - 1, 0)]))
    def _():
        w1b[...] = w1_ref[0, 0].astype(BF16)
        w3b[...] = w3_ref[0, 0].astype(BF16)
        w2b[...] = w2_ref[0, 0].astype(BF16)

    @pl.when(i < nu_ref[0])
    def _():
        x = x_ref[...]
        a = _mm(x, w1b[...])
        b = _mm(x, w3b[...])
        he = (a * _sigmoid(a) * b).astype(BF16)
        y_ref[...] = _mm(he, w2b[...]).astype(y_ref.dtype)

    @pl.when(i >= nu_ref[0])
    def _():
        y_ref[...] = jnp.zeros(y_ref.shape, y_ref.dtype)


def _ffn(plan, xs, w1, w3, w2, layer):
    rows, d = xs.shape
    f = w1.shape[3]
    grid_spec = pltpu.PrefetchScalarGridSpec(
        num_scalar_prefetch=3,
        grid=(rows // TF,),
        in_specs=[pl.BlockSpec((TF, d), lambda i, tile, te, nu: (tile[i], 0)),
                  pl.BlockSpec((1, 1, d, f), lambda i, tile, te, nu: (layer, te[i], 0, 0)),
                  pl.BlockSpec((1, 1, d, f), lambda i, tile, te, nu: (layer, te[i], 0, 0)),
                  pl.BlockSpec((1, 1, f, d), lambda i, tile, te, nu: (layer, te[i], 0, 0))],
        out_specs=pl.BlockSpec((TF, d), lambda i, tile, te, nu: (i, 0)),
        scratch_shapes=[pltpu.VMEM((d, f), BF16), pltpu.VMEM((d, f), BF16), pltpu.VMEM((f, d), BF16)],
    )
    return pl.pallas_call(
        _ffn_kernel,
        grid_spec=grid_spec,
        out_shape=jax.ShapeDtypeStruct((rows, d), BF16),
        compiler_params=pltpu.CompilerParams(dimension_semantics=("arbitrary",),
                                             vmem_limit_bytes=VMEM_LIMIT),
        name="moe_ffn",
    )(plan["tile"], plan["tile_expert"], plan["n_used"], xs, w1, w3, w2)


def _combine_kernel(*refs, final, ptd):
    cnt_ref, loff_ref, goff_ref, route_ref, ys_hbm, xn_ref, mod_ref = refs[:7]
    if final:
        fg_ref, yp_ref, ysm_ref, stage, sem = refs[7:]
    else:
        xo_ref, stage, sem = refs[7:]
    i = pl.program_id(0)
    n = pl.num_programs(0)
    slot = i % 2
    copies = functools.partial(_chunk_copies, cnt_ref, loff_ref, goff_ref, to_sorted=False)

    @pl.when(i == 0)
    def _():
        stage[...] = jnp.zeros(stage.shape, stage.dtype)
        copies(i, stage.at[0], ys_hbm, sem.at[0], wait=False)

    @pl.when(i + 1 < n)
    def _():
        copies(i + 1, stage.at[1 - slot], ys_hbm, sem.at[1 - slot], wait=False)

    route = route_ref[...]
    p0, p1 = _perm(route, loff_ref, i)
    gate_rows = jnp.where(p0, route[2:3], jnp.where(p1, route[3:4], 0.0))
    gate_col = jnp.sum(gate_rows, axis=1, keepdims=True)
    p = jnp.where(p0, 1.0, jnp.where(p1, 1.0, 0.0)).astype(BF16)
    copies(i, stage.at[slot], ys_hbm, sem.at[slot], wait=True)
    scaled = (stage[slot].astype(F32) * gate_col).astype(BF16)
    y = lax.dot_general(p, scaled, (((0,), (0,)), ((), ())), preferred_element_type=F32)
    xo = xn_ref[...] + mod_ref[0, 0][5:6] * y
    if final:
        out = _rms(xo, fg_ref[...])

        @pl.when(i < ptd)
        def _():
            yp_ref[...] = out

        @pl.when(i >= ptd)
        def _():
            ysm_ref[...] = out
    else:
        xo_ref[...] = xo


def _combine(plan, route, ys, xn, mod, final_g, *, layer, final, p_tok, s_seq):
    t_tok, d = xn.shape
    nd = t_tok // TD
    ptd = p_tok // TD
    std = s_seq // TD
    in_specs = [pl.BlockSpec((8, TD), lambda i, *_: (0, i)),
                pl.BlockSpec(memory_space=pl.ANY),
                pl.BlockSpec((TD, d), lambda i, *_: (i, 0)),
                pl.BlockSpec((1, 1, N_MOD, d),
                             lambda i, *_: (layer, jnp.where(i < ptd, 0, 1 + (i - ptd) // std), 0, 0))]
    args = [route, ys, xn, mod]
    if final:
        in_specs.append(pl.BlockSpec((1, d), lambda i, *_: (0, 0)))
        args.append(final_g)
        out_specs = [pl.BlockSpec((TD, d), lambda i, *_: (jnp.minimum(i, ptd - 1), 0)),
                     pl.BlockSpec((TD, d), lambda i, *_: (jnp.maximum(i - ptd, 0), 0))]
        out_shape = [jax.ShapeDtypeStruct((p_tok, d), F32),
                     jax.ShapeDtypeStruct((t_tok - p_tok, d), F32)]
    else:
        out_specs = pl.BlockSpec((TD, d), lambda i, *_: (i, 0))
        out_shape = jax.ShapeDtypeStruct((t_tok, d), F32)
    grid_spec = pltpu.PrefetchScalarGridSpec(
        num_scalar_prefetch=3, grid=(nd,), in_specs=in_specs, out_specs=out_specs,
        scratch_shapes=[pltpu.VMEM((2, _stage_rows(), d), BF16), pltpu.SemaphoreType.DMA((2,))])
    return pl.pallas_call(
        functools.partial(_combine_kernel, final=final, ptd=ptd),
        grid_spec=grid_spec, out_shape=out_shape,
        compiler_params=pltpu.CompilerParams(dimension_semantics=("arbitrary",),
                                             vmem_limit_bytes=VMEM_LIMIT),
        name="moe_combine_final" if final else "moe_combine",
    )(plan["cnt16"], plan["loff"], plan["goff"], *args)


def kernel(x_prompt, x_sample, cache_even_k, cache_even_v, cache_odd_k, cache_odd_v, c, c_ctx, w_ada, b_ada, norm_g, final_g, even_w_in, even_w_out, even_lam, even_subln_g, even_conv_w, even_conv_b, even_conv_norm_g, even_conv_norm_b, odd_w_qkv, odd_w_out, odd_rpb, router_w, router_b, moe_w1, moe_w3, moe_w2):
    batch, seq, d = x_prompt.shape
    dbatch, dseq, _ = x_sample.shape
    past = cache_even_k.shape[2]
    depth = w_ada.shape[0]
    p_tok, s_tok = batch * seq, dbatch * dseq
    t_tok = p_tok + s_tok
    assert TM % seq == 0 and seq % CONV_ROWS == 0 and seq >= HALO
    assert dseq % TM == 0 and dseq % TD == 0 and p_tok % TM == 0 and p_tok % TD == 0 and p_tok % dseq == 0
    assert 1 + dbatch <= COND_ROWS and dseq % GRID_W == 0 and past % TD == 0
    assert dseq // GRID_W >= NA_KROWS + NA_QROWS

    cond = jnp.zeros((COND_ROWS, d), F32).at[0].set(c_ctx).at[1:1 + dbatch].set(c)
    mod = _ada(cond, w_ada, b_ada).reshape(depth, COND_ROWS, N_MOD, d)
    rope = _rope_tables(dseq)
    rwh, rwl = _split_bf16(jnp.pad(router_w, ((0, 0), (0, LANE - N_EXPERTS))))
    rw2 = jnp.concatenate([rwh, rwl], axis=1)
    rbias = router_b.reshape(N_EXPERTS, 1)
    xs = [x_prompt.reshape(p_tok, d), x_sample.reshape(s_tok, d)]
    outs = {}

    for layer in range(depth):
        j = layer // 2
        even = layer % 2 == 0
        g1 = norm_g[layer, 0].reshape(1, d)
        g2 = norm_g[layer, 1].reshape(1, d)
        if even:
            q, k, v, u, kf, vf = _pre(xs, mod, g1, even_w_in[j].astype(BF16), rope, layer=layer,
                                      even=True, p_tok=p_tok, p_seq=seq, s_seq=dseq,
                                      heads=(A_HEADS, A_V_DIM))
            outs["even_k"], outs["even_v"] = kf, vf
            lam_init = 0.8 - 0.6 * float(np.exp(-0.3 * layer))
            lam = even_lam[j]
            subln = even_subln_g[j].reshape(1, A_V_DIM)
            ap = _pair_attn(q, k, v, batch=batch, n_pairs=A_HEADS, nq_tok=seq, nk_tok=seq, q_row0=0,
                            k_row0=0, tq=seq, tk=seq, groups=PROMPT_GROUPS, diff=True, lam=lam,
                            subln=subln, lam_init=lam_init)
            ctx = (cache_even_k.reshape(dbatch, -1, past, A_WIDTH),
                   cache_even_v.reshape(dbatch, -1, past, A_WIDTH), j)
            a_s = _pair_attn(q, k, v, batch=dbatch, n_pairs=A_HEADS, nq_tok=dseq, nk_tok=dseq,
                             q_row0=p_tok, k_row0=p_tok, tq=ATT_TQ, tk=ATT_TK, groups=1, ctx=ctx, diff=True,
                             lam=lam, subln=subln, lam_init=lam_init)
            conv = [even_conv_w[j], even_conv_b[j].reshape(1, -1), even_conv_norm_g[j].reshape(1, -1),
                    even_conv_norm_b[j].reshape(1, -1)]
            xn, h2, logits = _post((ap, a_s), u, conv, even_w_out[j].astype(BF16), xs, mod, g2, rw2,
                                   layer=layer, even=True, p_tok=p_tok, p_seq=seq, s_seq=dseq)
        else:
            q, k, v, kf, vf = _pre(xs, mod, g1, odd_w_qkv[j].astype(BF16), None, layer=layer,
                                   even=False, p_tok=p_tok, p_seq=seq, s_seq=dseq,
                                   heads=(C_HEADS, C_HEAD_DIM))
            outs["odd_k"] = kf.reshape(batch, 1, seq, C_HEADS, C_HEAD_DIM)
            outs["odd_v"] = vf.reshape(batch, 1, seq, C_HEADS, C_HEAD_DIM)
            n_pairs = d // LANE
            op = _pair_attn(q, k, v, batch=batch, n_pairs=n_pairs, nq_tok=seq, nk_tok=seq, q_row0=0,
                            k_row0=0, tq=seq, tk=seq, groups=PROMPT_GROUPS)
            bias = _na_bias(odd_rpb[j])
            o_s = _na(q, k, v, cache_odd_k.reshape(dbatch, -1, past, d),
                      cache_odd_v.reshape(dbatch, -1, past, d), j, bias,
                      batch=dbatch, n_tok=dseq, row0=p_tok)
            xn, h2, logits = _post((op, o_s), None, None, odd_w_out[j].astype(BF16), xs, mod, g2, rw2,
                                   layer=layer, even=False, p_tok=p_tok, p_seq=seq, s_seq=dseq)
        route, cnt = _route(logits, rbias)
        plan = _moe_plan(cnt, t_tok)
        x_sorted = _dispatch(plan, route, h2)
        y_sorted = _ffn(plan, x_sorted, moe_w1, moe_w3, moe_w2, layer)
        final = layer == depth - 1
        res = _combine(plan, route, y_sorted, xn, mod, final_g.reshape(1, d), layer=layer, final=final,
                       p_tok=p_tok, s_seq=dseq)
        if final:
            y_prompt, y_sample = res
        else:
            xs = [res]

    return (y_prompt.reshape(batch, seq, d), y_sample.reshape(dbatch, dseq, d),
            outs["even_k"], outs["even_v"], outs["odd_k"], outs["odd_v"])
```

```python
import functools

import numpy as np
import jax
import jax.numpy as jnp
from jax import lax
from jax.experimental import pallas as pl
from jax.experimental.pallas import tpu as pltpu

F32 = jnp.float32
BF16 = jnp.bfloat16
I32 = jnp.int32

GRID_W = 64
EPS = 1e-6
N_MOD = 6
A_HEADS = 4
A_QK_DIM = 64
A_V_DIM = 128
A_WIDTH = A_HEADS * A_V_DIM
CONV_WIDTH = 31
ROPE_BASE = 10000.0
ROPE_FREQS = A_QK_DIM // 4
C_HEADS = 16
C_HEAD_DIM = 64
NA_ROWS = 8
NA_COLS = 16
N_EXPERTS = 16
N_GROUPS = 4
EXPERTS_PER_GROUP = N_EXPERTS // N_GROUPS
NEG = -1e30
LOG2E = 1.4426950408889634

LANE = 128
SUBLANE = 8
BF16_ROWS = 16
VMEM_LIMIT = 52 * 1024 * 1024

TM = 512
ATT_TQ = 512
ATT_TK = 512
ATT_GROUPS = 2
TD = 512
TF = 512
HALO = 16
CONV_ROWS = 64
ROUTE_TILES = 8
NA_QROWS = 4
NA_KROWS = NA_QROWS + NA_ROWS
NA_GROUPS = 16
PROMPT_GROUPS = 4
COND_ROWS = 16


def _nt(a, b):
    return lax.dot_general(a, b, (((1,), (1,)), ((), ())), preferred_element_type=F32)


def _mm(a, b):
    return jnp.dot(a, b, preferred_element_type=F32)


def _split_bf16(a):
    hi = a.astype(BF16)
    lo = (a - hi.astype(F32)).astype(BF16)
    return hi, lo


def _sigmoid(x):
    return 1.0 / (1.0 + jnp.exp(-x))


def _rms(x, g):
    return x * lax.rsqrt(jnp.mean(x * x, axis=-1, keepdims=True) + EPS) * g


def _ada_kernel(cond_ref, w_ref, b_ref, o_ref):
    c = cond_ref[...]
    a = c * _sigmoid(c)
    ah, al = _split_bf16(a)
    wh, wl = _split_bf16(w_ref[0])
    o_ref[0] = _mm(ah, wh) + _mm(ah, wl) + _mm(al, wh) + b_ref[0]


def _ada(cond, w_ada, b_ada):
    depth, d, n = w_ada.shape
    tn = 1536
    return pl.pallas_call(
        _ada_kernel,
        grid=(depth, n // tn),
        in_specs=[pl.BlockSpec((COND_ROWS, d), lambda l, j: (0, 0)),
                  pl.BlockSpec((1, d, tn), lambda l, j: (l, 0, j)),
                  pl.BlockSpec((1, 1, tn), lambda l, j: (l, 0, j))],
        out_specs=pl.BlockSpec((1, COND_ROWS, tn), lambda l, j: (l, 0, j)),
        out_shape=jax.ShapeDtypeStruct((depth, COND_ROWS, n), F32),
        compiler_params=pltpu.CompilerParams(dimension_semantics=("arbitrary", "arbitrary"),
                                             vmem_limit_bytes=VMEM_LIMIT),
        name="ada",
    )(cond, w_ada, b_ada.reshape(depth, 1, n))


def _load_x(x_refs, t, pt):
    if len(x_refs) == 1:
        return x_refs[0][...]
    return jnp.where(t < pt, x_refs[0][...], x_refs[1][...])


def _store_heads(ref, val):
    if len(ref.shape) == 2:
        ref[...] = val
        return
    nb, _, seq, heads, hd = ref.shape
    for b in range(nb):
        for h in range(heads):
            ref[b, 0, :, h, :] = val[b * seq:(b + 1) * seq, h * hd:(h + 1) * hd]


def _pre_even_kernel(*refs, n_x, pt):
    x_refs = refs[:n_x]
    (mod_ref, g_ref, w_ref, c_ref, s1_ref, s2_ref,
     q_ref, k_ref, v_ref, u_ref, kf_ref, vf_ref) = refs[n_x:]
    t = pl.program_id(0)
    x = _load_x(x_refs, t, pt)
    mod = mod_ref[0, 0]
    h = _rms(x, g_ref[...]) * (1.0 + mod[1:2]) + mod[0:1]
    proj = _mm(h.astype(BF16), w_ref[...])
    aw = A_WIDTH
    q = proj[:, :aw]
    k = proj[:, aw:2 * aw]
    v = proj[:, 2 * aw:3 * aw]
    bw = (proj.shape[1] - 3 * aw) // 2
    gv = proj[:, 3 * aw:3 * aw + bw]
    gg = proj[:, 3 * aw + bw:]
    v_ref[...] = v.astype(BF16)
    u_ref[...] = (gv * _sigmoid(gg)).astype(BF16)
    scale = A_QK_DIM ** -0.5 * LOG2E

    @pl.when(t < pt)
    def _():
        q_ref[...] = (q * scale).astype(BF16)
        k_ref[...] = k.astype(BF16)
        _store_heads(kf_ref, k)
        _store_heads(vf_ref, v)

    @pl.when(t >= pt)
    def _():
        reps = aw // LANE
        c = jnp.concatenate([c_ref[...]] * reps, axis=1)
        s1 = jnp.concatenate([s1_ref[...]] * reps, axis=1)
        s2 = jnp.concatenate([s2_ref[...]] * reps, axis=1)

        def rope(z):
            return (z * c + pltpu.roll(z, ROPE_FREQS, 1) * s1
                    + pltpu.roll(z, aw - ROPE_FREQS, 1) * s2)

        q_ref[...] = (rope(q) * scale).astype(BF16)
        k_ref[...] = rope(k).astype(BF16)


def _pre_odd_kernel(*refs, n_x, pt):
    x_refs = refs[:n_x]
    mod_ref, g_ref, w_ref, q_ref, k_ref, v_ref, kf_ref, vf_ref = refs[n_x:]
    t = pl.program_id(0)
    x = _load_x(x_refs, t, pt)
    mod = mod_ref[0, 0]
    h = _rms(x, g_ref[...]) * (1.0 + mod[1:2]) + mod[0:1]
    proj = _mm(h.astype(BF16), w_ref[...])
    d = proj.shape[1] // 3
    q = proj[:, :d]
    k = proj[:, d:2 * d]
    v = proj[:, 2 * d:]
    q_ref[...] = (q * (C_HEAD_DIM ** -0.5 * LOG2E)).astype(BF16)
    k_ref[...] = k.astype(BF16)
    v_ref[...] = v.astype(BF16)

    @pl.when(t < pt)
    def _():
        _store_heads(kf_ref, k)
        _store_heads(vf_ref, v)


def _x_specs(xs, pt, d):
    if len(xs) == 1:
        return [pl.BlockSpec((TM, d), lambda t: (t, 0))]
    return [pl.BlockSpec((TM, d), lambda t: (jnp.minimum(t, pt - 1), 0)),
            pl.BlockSpec((TM, d), lambda t: (jnp.maximum(t - pt, 0), 0))]


def _pre(xs, mod, g, w, rope, *, layer, even, p_tok, p_seq, s_seq, heads):
    d = xs[0].shape[1]
    nb = TM // p_seq
    if heads[1] % LANE == 0:
        cache_spec = pl.BlockSpec((nb, 1, p_seq) + heads, lambda t: (jnp.minimum(t, pt - 1), 0, 0, 0, 0))
        cache_shape = jax.ShapeDtypeStruct((p_tok // p_seq, 1, p_seq) + heads, F32)
    else:
        cache_spec = pl.BlockSpec((TM, d), lambda t: (jnp.minimum(t, pt - 1), 0))
        cache_shape = jax.ShapeDtypeStruct((p_tok, d), F32)
    t_tok = sum(x.shape[0] for x in xs) if len(xs) == 2 else xs[0].shape[0]
    nt = t_tok // TM
    pt = p_tok // TM
    st = s_seq // TM
    x_specs = _x_specs(xs, pt, d)
    mod_spec = pl.BlockSpec((1, 1, N_MOD, d),
                            lambda t: (layer, jnp.where(t < pt, 0, 1 + (t - pt) // st), 0, 0))
    g_spec = pl.BlockSpec((1, d), lambda t: (0, 0))
    w_spec = pl.BlockSpec(w.shape, lambda t: (0, 0))
    tok = lambda width: pl.BlockSpec((TM, width), lambda t: (t, 0))
    params = pltpu.CompilerParams(dimension_semantics=("arbitrary",), vmem_limit_bytes=VMEM_LIMIT)
    if even:
        aw = A_WIDTH
        bw = (w.shape[1] - 3 * aw) // 2
        rspec = pl.BlockSpec((TM, LANE), lambda t: (jnp.maximum(t - pt, 0) % st, 0))
        return pl.pallas_call(
            functools.partial(_pre_even_kernel, n_x=len(xs), pt=pt),
            grid=(nt,),
            in_specs=x_specs + [mod_spec, g_spec, w_spec, rspec, rspec, rspec],
            out_specs=[tok(aw), tok(aw), tok(aw), tok(bw), cache_spec, cache_spec],
            out_shape=[jax.ShapeDtypeStruct((t_tok, aw), BF16)] * 3
            + [jax.ShapeDtypeStruct((t_tok, bw), BF16)] + [cache_shape] * 2,
            compiler_params=params, name="pre_even",
        )(*xs, mod, g, w, *rope)
    return pl.pallas_call(
        functools.partial(_pre_odd_kernel, n_x=len(xs), pt=pt),
        grid=(nt,),
        in_specs=x_specs + [mod_spec, g_spec, w_spec],
        out_specs=[tok(d), tok(d), tok(d), cache_spec, cache_spec],
        out_shape=[jax.ShapeDtypeStruct((t_tok, d), BF16)] * 3 + [cache_shape] * 2,
        compiler_params=params, name="pre_odd",
    )(*xs, mod, g, w)


def _rope_tables(n_tokens):
    t = jnp.arange(n_tokens, dtype=I32)
    pos = jnp.stack([t // GRID_W, t % GRID_W], axis=-1).astype(F32)
    freqs = ROPE_BASE ** (-jnp.arange(ROPE_FREQS, dtype=F32) / ROPE_FREQS)
    ang = pos[:, :, None] * freqs
    cos, sin = jnp.cos(ang), jnp.sin(ang)
    zero = jnp.zeros_like(sin)
    c = jnp.concatenate([cos, cos], axis=-1).reshape(n_tokens, A_QK_DIM)
    s1 = jnp.concatenate([zero, sin], axis=-1).reshape(n_tokens, A_QK_DIM)
    s2 = jnp.concatenate([-sin, zero], axis=-1).reshape(n_tokens, A_QK_DIM)
    rep = LANE // A_QK_DIM
    return tuple(jnp.tile(a, (1, rep)) for a in (c, s1, s2))


def _stack_maps(q):
    lane = lax.broadcasted_iota(I32, q.shape, 1)
    zero = jnp.zeros_like(q)
    half = LANE // 2
    return jnp.concatenate([jnp.where(lane < half, q, zero), jnp.where(lane >= half, q, zero)], axis=0)


def _pair_attn_kernel(*refs, diff, ctx, groups, tq, nk, tk, lam_init):
    refs = list(refs)
    q_ref = refs.pop(0)
    kc_ref, vc_ref = (refs.pop(0), refs.pop(0)) if ctx else (None, None)
    k_ref, v_ref = refs.pop(0), refs.pop(0)
    lam_ref, sg_ref = (refs.pop(0), refs.pop(0)) if diff else (None, None)
    o_ref = refs.pop(0)
    half = LANE // 2
    for g in range(groups):
        qs = _stack_maps(q_ref[g * tq:(g + 1) * tq, :])
        chunks = []
        if ctx:
            for j in range(kc_ref.shape[2] // tk):
                rows = slice(j * tk, (j + 1) * tk)
                chunks.append((lambda rows=rows: kc_ref[0, 0, rows, :].astype(BF16),
                               lambda rows=rows: vc_ref[0, 0, rows, :].astype(BF16)))
        for j in range(nk):
            first = j if ctx else g * nk + j
            rows = slice(first * tk, (first + 1) * tk)
            chunks.append((lambda rows=rows: k_ref[rows, :], lambda rows=rows: v_ref[rows, :]))
        ones = jnp.ones((tk, LANE), BF16)
        m = acc = None
        for j, (get_k, get_v) in enumerate(chunks):
            s = _nt(qs, get_k())
            mc = jnp.max(s, axis=-1, keepdims=True)
            vx = jnp.concatenate([get_v(), ones], axis=1)
            if j == 0:
                m = mc
                acc = _mm(jnp.exp2(s - m).astype(BF16), vx)
            else:
                m_new = jnp.maximum(m, mc)
                alpha = jnp.exp2(m - m_new)
                acc = alpha * acc + _mm(jnp.exp2(s - m_new).astype(BF16), vx)
                m = m_new
        o = acc[:, :LANE] / acc[:, LANE:]
        o1 = o[0:tq]
        o2 = o[tq:]
        if diff:
            lf = lam_ref[...]
            lam = (jnp.exp(jnp.sum(lf[0:1] * lf[1:2], axis=1, keepdims=True))
                   - jnp.exp(jnp.sum(lf[2:3] * lf[3:4], axis=1, keepdims=True)) + lam_init)
            dlt = o1 - lam * o2
            out = _rms(dlt, sg_ref[...]) * (1.0 - lam_init)
        else:
            lane = lax.broadcasted_iota(I32, o1.shape, 1)
            out = jnp.where(lane < half, o1, o2)
        o_ref[g * tq:(g + 1) * tq, :] = out.astype(o_ref.dtype)


def _pair_attn(q, k, v, *, batch, n_pairs, nq_tok, nk_tok, q_row0, k_row0, tq, tk, groups,
               ctx=None, diff=False, lam=None, subln=None, lam_init=0.0):
    nk = nk_tok // tk
    if ctx is not None:
        assert nq_tok % (groups * tq) == 0
        nb, nq = batch, nq_tok // (groups * tq)
        qblk, kblk = groups * tq, nk_tok
    else:
        assert groups == 1 or (nq_tok == tq and batch % groups == 0)
        nb, nq = (batch // groups, 1) if groups > 1 else (batch, nq_tok // tq)
        qblk, kblk = groups * tq, groups * nk_tok
    qb0, kb0 = q_row0 // qblk, k_row0 // kblk
    in_specs = [pl.BlockSpec((qblk, LANE), lambda b, h, qi: (qb0 + b * nq + qi, h))]
    args = [q]
    if ctx is not None:
        ck, cv, cj = ctx
        past = ck.shape[2]
        assert past % tk == 0
        in_specs += [pl.BlockSpec((1, 1, past, LANE), lambda b, h, qi: (b, cj, 0, h))] * 2
        args += [ck, cv]
    in_specs += [pl.BlockSpec((kblk, LANE), lambda b, h, qi: (kb0 + b, h))] * 2
    args += [k, v]
    if diff:
        in_specs += [pl.BlockSpec(lam.shape, lambda b, h, qi: (0, 0)),
                     pl.BlockSpec(subln.shape, lambda b, h, qi: (0, 0))]
        args += [lam, subln]
    return pl.pallas_call(
        functools.partial(_pair_attn_kernel, diff=diff, ctx=ctx is not None, groups=groups, tq=tq,
                          nk=nk, tk=tk, lam_init=lam_init),
        grid=(nb, n_pairs, nq),
        in_specs=in_specs,
        out_specs=pl.BlockSpec((qblk, LANE), lambda b, h, qi: (b * nq + qi, h)),
        out_shape=jax.ShapeDtypeStruct((batch * nq_tok, n_pairs * LANE), BF16),
        compiler_params=pltpu.CompilerParams(
            dimension_semantics=("arbitrary",) * 3, vmem_limit_bytes=VMEM_LIMIT),
        name="diff_attn" if diff else "pair_attn",
    )(*args)


def _na_kernel(q_ref, k_ref, v_ref, kc_ref, vc_ref, colb_ref, o_ref, bias_ref, *, nrb, n_tok, groups):
    tq = NA_QROWS * GRID_W
    tkw = NA_KROWS * GRID_W
    half = LANE // 2
    rows = n_tok // GRID_W

    @pl.when(jnp.logical_and(pl.program_id(1) == 0, pl.program_id(2) == 0))
    def _():
        lane = lax.broadcasted_iota(I32, (GRID_W, LANE), 1)
        masked = jnp.full((GRID_W, LANE), NEG * LOG2E, F32)
        for x, rb in enumerate((0, 1, nrb - 1)):
            start = min(max(NA_QROWS * rb - NA_QROWS, 0), rows - NA_KROWS)
            for hh in range(2):
                for i in range(NA_QROWS):
                    r = NA_QROWS * rb + i
                    rs = min(max(r - NA_ROWS // 2, 0), rows - NA_ROWS)
                    row0 = (hh * NA_QROWS + i) * GRID_W
                    for jp in range(NA_KROWS // 2):
                        kr = start + 2 * jp
                        ok_l = rs <= kr < rs + NA_ROWS
                        ok_r = rs <= kr + 1 < rs + NA_ROWS
                        if ok_l or ok_r:
                            blk = colb_ref[hh, kr - r + NA_ROWS]
                            if not ok_l:
                                blk = jnp.where(lane >= half, blk, masked)
                            if not ok_r:
                                blk = jnp.where(lane < half, blk, masked)
                        else:
                            blk = masked
                        bias_ref[x, row0:row0 + GRID_W, jp * LANE:(jp + 1) * LANE] = blk

    kc = kc_ref[0, 0].astype(BF16)
    vc = jnp.concatenate([vc_ref[0, 0].astype(BF16), jnp.ones((kc.shape[0], LANE), BF16)], axis=1)
    ones = jnp.ones((tkw, LANE), BF16)
    for g in range(groups):
        rb = pl.program_id(2) * groups + g
        qs = _stack_maps(q_ref[g * tq:(g + 1) * tq, :])
        start = pl.multiple_of(jnp.clip(rb * tq - tq, 0, n_tok - tkw), tq)
        case = jnp.where(rb == 0, 0, jnp.where(rb == nrb - 1, 2, 1))
        s_loc = _nt(qs, k_ref[pl.ds(start, tkw), :]) + bias_ref[case]
        s_ctx = _nt(qs, kc)
        m = jnp.maximum(jnp.max(s_loc, axis=-1, keepdims=True), jnp.max(s_ctx, axis=-1, keepdims=True))
        v_loc = jnp.concatenate([v_ref[pl.ds(start, tkw), :], ones], axis=1)
        acc = _mm(jnp.exp2(s_loc - m).astype(BF16), v_loc) + _mm(jnp.exp2(s_ctx - m).astype(BF16), vc)
        o = acc[:, :LANE] / acc[:, LANE:]
        lane = lax.broadcasted_iota(I32, (tq, LANE), 1)
        o_ref[g * tq:(g + 1) * tq, :] = jnp.where(lane < half, o[0:tq], o[tq:]).astype(o_ref.dtype)


def _na_bias(rpb):
    h = rpb.shape[0]
    nd = 2 * NA_COLS - 1
    c = np.arange(GRID_W)[:, None]
    kc = np.arange(GRID_W)[None, :]
    cs = np.clip(c - NA_COLS // 2, 0, GRID_W - NA_COLS)
    okc = (kc >= cs) & (kc < cs + NA_COLS)
    dc = np.where(okc, kc - c + NA_COLS - 1, -1)
    onehot = (dc.reshape(1, -1) == np.arange(nd)[:, None]).astype(np.float32)
    colmask = np.where(okc, 0.0, NEG).astype(np.float32).reshape(1, 1, GRID_W, GRID_W)
    colb = jnp.dot(rpb.astype(F32).reshape(-1, nd), onehot, precision=lax.Precision.HIGHEST)
    colb = (colb.reshape(h, 2 * NA_ROWS - 1, GRID_W, GRID_W) + colmask) * LOG2E
    masked = jnp.full((h, 1, GRID_W, GRID_W), NEG * LOG2E, F32)
    colb = jnp.concatenate([masked, colb, masked], axis=1)
    return jnp.concatenate([colb[:, :-1], colb[:, 1:]], axis=-1)


def _na(q, k, v, kc, vc, cj, colb, *, batch, n_tok, row0):
    d = q.shape[1]
    n_pairs = d // LANE
    tq = NA_QROWS * GRID_W
    nrb = n_tok // tq
    past = kc.shape[2]
    groups = NA_GROUPS
    while nrb % groups:
        groups //= 2
    qblk = groups * tq
    ns = nrb // groups
    qb0 = row0 // qblk
    kb0 = row0 // n_tok
    return pl.pallas_call(
        functools.partial(_na_kernel, nrb=nrb, n_tok=n_tok, groups=groups),
        grid=(n_pairs, batch, ns),
        in_specs=[pl.BlockSpec((qblk, LANE), lambda h, b, r: (qb0 + b * ns + r, h)),
                  pl.BlockSpec((n_tok, LANE), lambda h, b, r: (kb0 + b, h)),
                  pl.BlockSpec((n_tok, LANE), lambda h, b, r: (kb0 + b, h)),
                  pl.BlockSpec((1, 1, past, LANE), lambda h, b, r: (b, cj, 0, h)),
                  pl.BlockSpec((1, 1, past, LANE), lambda h, b, r: (b, cj, 0, h)),
                  pl.BlockSpec((2,) + colb.shape[1:], lambda h, b, r: (h, 0, 0, 0))],
        out_specs=pl.BlockSpec((qblk, LANE), lambda h, b, r: (b * ns + r, h)),
        out_shape=jax.ShapeDtypeStruct((batch * n_tok, d), BF16),
        scratch_shapes=[pltpu.VMEM((3, 2 * tq, NA_KROWS * GRID_W), F32)],
        compiler_params=pltpu.CompilerParams(
            dimension_semantics=("arbitrary",) * 3, vmem_limit_bytes=VMEM_LIMIT),
        name="na_attn",
    )(q, k, v, kc, vc, colb)


def _route_kernel(logit_ref, rb_ref, route_ref, cnt_ref):
    parts = [logit_ref[j] for j in range(logit_ref.shape[0])]
    logits = parts[0] if len(parts) == 1 else jnp.concatenate(parts, axis=1)
    scores = _sigmoid(logits)
    sel = scores + rb_ref[...]
    tm = sel.shape[1]
    gs = []
    for g in range(N_GROUPS):
        r = [sel[g * EXPERTS_PER_GROUP + a:g * EXPERTS_PER_GROUP + a + 1] for a in range(EXPERTS_PER_GROUP)]
        best = None
        for a in range(EXPERTS_PER_GROUP):
            for b in range(a + 1, EXPERTS_PER_GROUP):
                pair = r[a] + r[b]
                best = pair if best is None else jnp.maximum(best, pair)
        gs.append(best)
    bg = jnp.zeros((1, tm), I32)
    bs = gs[0]
    for g in range(1, N_GROUPS):
        better = gs[g] > bs
        bg = jnp.where(better, g, bg)
        bs = jnp.where(better, gs[g], bs)
    eidx = lax.broadcasted_iota(I32, sel.shape, 0)
    masked = jnp.where(eidx // EXPERTS_PER_GROUP == bg, sel, -jnp.inf)
    m1 = jnp.max(masked, axis=0, keepdims=True)
    i1 = jnp.min(jnp.where(masked == m1, eidx, N_EXPERTS), axis=0, keepdims=True)
    masked2 = jnp.where(eidx == i1, -jnp.inf, masked)
    m2 = jnp.max(masked2, axis=0, keepdims=True)
    i2 = jnp.min(jnp.where(masked2 == m2, eidx, N_EXPERTS), axis=0, keepdims=True)
    s1 = jnp.sum(jnp.where(eidx == i1, scores, 0.0), axis=0, keepdims=True)
    s2 = jnp.sum(jnp.where(eidx == i2, scores, 0.0), axis=0, keepdims=True)
    tot = s1 + s2
    route_ref[...] = jnp.concatenate(
        [i1.astype(F32), i2.astype(F32), s1 / tot, s2 / tot, jnp.zeros((4, tm), F32)], axis=0)
    chosen = jnp.where(eidx == i1, 1.0, jnp.where(eidx == i2, 1.0, 0.0))
    lane = lax.broadcasted_iota(I32, (N_EXPERTS, LANE), 1)
    cnt = jnp.zeros((N_EXPERTS, LANE), F32)
    for j in range(tm // TD):
        cj = jnp.sum(chosen[:, j * TD:(j + 1) * TD], axis=1, keepdims=True)
        cnt = jnp.where(lane == j, cj, cnt)
    cnt_ref[0] = cnt


def _route(logits, rbias):
    nt = logits.shape[0]
    t_tok = nt * TM
    step = ROUTE_TILES
    while nt % step:
        step //= 2
    span = step * TM
    assert span % TD == 0
    route, cnt = pl.pallas_call(
        _route_kernel,
        grid=(nt // step,),
        in_specs=[pl.BlockSpec((step, N_EXPERTS, TM), lambda i: (i, 0, 0)),
                  pl.BlockSpec(rbias.shape, lambda i: (0, 0))],
        out_specs=[pl.BlockSpec((8, span), lambda i: (0, i)),
                   pl.BlockSpec((1, N_EXPERTS, LANE), lambda i: (i, 0, 0))],
        out_shape=[jax.ShapeDtypeStruct((8, t_tok), F32),
                   jax.ShapeDtypeStruct((nt // step, N_EXPERTS, LANE), F32)],
        compiler_params=pltpu.CompilerParams(dimension_semantics=("arbitrary",),
                                             vmem_limit_bytes=VMEM_LIMIT),
        name="route",
    )(logits, rbias)
    per = span // TD
    cnt = cnt[:, :, :per].astype(I32).transpose(0, 2, 1).reshape(t_tok // TD, N_EXPERTS)
    return route, cnt


def _post_kernel(*refs, even, n_x, pt, st):
    a_refs, refs = refs[:2], refs[2:]
    if even:
        (up_ref, uc_ref, un_ref, cw_ref, cb_ref, cg_ref, cnb_ref, wo_ref) = refs[:8]
        rest = refs[8:]
    else:
        wo_ref = refs[0]
        rest = refs[1:]
    x_refs = rest[:n_x]
    (mod_ref, g2_ref, rw2_ref, xn_ref, h2_ref, logit_ref) = rest[n_x:n_x + 6]
    t = pl.program_id(0)
    a = _load_x(a_refs, t, pt)
    if even:
        ext, shifted = rest[n_x + 6:]
        n_sub = ext.shape[0]
        sub = TM // n_sub
        latent = t >= pt
        js = jnp.maximum(t - pt, 0) % st
        pad = CONV_WIDTH // 2
        span = shifted.shape[2]
        blocks = []
        for s in range(n_sub):
            if s == 0:
                prev, has_prev = up_ref[...], jnp.logical_and(latent, js != 0)
            else:
                prev, has_prev = uc_ref[s * sub - HALO:s * sub, :], latent
            if s == n_sub - 1:
                nxt, has_next = un_ref[...], jnp.logical_and(latent, js != st - 1)
            else:
                nxt, has_next = uc_ref[(s + 1) * sub:(s + 1) * sub + HALO, :], latent
            prev = prev.astype(F32)
            nxt = nxt.astype(F32)
            ext[s, 0:HALO, :] = jnp.where(has_prev, prev, jnp.zeros_like(prev))
            ext[s, HALO:HALO + sub, :] = uc_ref[s * sub:(s + 1) * sub, :].astype(F32)
            ext[s, HALO + sub:, :] = jnp.where(has_next, nxt, jnp.zeros_like(nxt))
            for ph in range(1, SUBLANE):
                shifted[s, ph - 1] = ext[s, pl.ds(ph, span), :]
            for r0 in range(0, sub, CONV_ROWS):
                acc = jnp.zeros((CONV_ROWS, ext.shape[2]), F32) + cb_ref[...]
                for j in range(CONV_WIDTH):
                    whole, ph = divmod(HALO - pad + j, SUBLANE)
                    rows = slice(r0 + whole * SUBLANE, r0 + whole * SUBLANE + CONV_ROWS)
                    tap = ext[s, rows, :] if ph == 0 else shifted[s, ph - 1, rows, :]
                    acc = acc + cw_ref[j:j + 1, :] * tap
                blocks.append(acc)
        acc = jnp.concatenate(blocks, axis=0)
        mu = jnp.mean(acc, axis=-1, keepdims=True)
        var = jnp.mean(jnp.square(acc - mu), axis=-1, keepdims=True)
        y = (acc - mu) * lax.rsqrt(var + EPS) * cg_ref[...] + cnb_ref[...]
        ua = (y * _sigmoid(y)).astype(BF16)
        aw = a.shape[1]
        mix = _mm(a, wo_ref[0:aw, :]) + _mm(ua, wo_ref[aw:, :])
    else:
        mix = _mm(a, wo_ref[...])
    x = _load_x(x_refs, t, pt)
    mod = mod_ref[0, 0]
    xn = x + mod[2:3] * mix
    xn_ref[...] = xn
    h2 = _rms(xn, g2_ref[...]) * (1.0 + mod[4:5]) + mod[3:4]
    h2_ref[...] = h2.astype(BF16)
    hh, hl = _split_bf16(h2)
    both = _mm(hh, rw2_ref[...])
    lt = both[:, :LANE] + both[:, LANE:] + _mm(hl, rw2_ref[:, 0:LANE])
    logit_ref[0] = lt.T[0:N_EXPERTS, :]


def _post(a_parts, u, conv, wo, xs, mod, g2, rw2, *, layer, even, p_tok, p_seq, s_seq):
    d = wo.shape[1]
    t_tok = a_parts[0].shape[0] + a_parts[1].shape[0]
    nt = t_tok // TM
    pt = p_tok // TM
    st = s_seq // TM
    tok = lambda width: pl.BlockSpec((TM, width), lambda t: (t, 0))
    full = lambda arr: pl.BlockSpec(arr.shape, lambda t: (0,) * arr.ndim)
    in_specs = _x_specs(a_parts, pt, a_parts[0].shape[1])
    args = list(a_parts)
    scratch = []
    if even:
        bw = u.shape[1]
        hb = TM // HALO
        in_specs += [pl.BlockSpec((HALO, bw), lambda t: (jnp.maximum(t * hb - 1, 0), 0)),
                     tok(bw),
                     pl.BlockSpec((HALO, bw), lambda t: (jnp.minimum((t + 1) * hb, nt * hb - 1), 0))]
        args += [u, u, u]
        for arr in conv:
            in_specs.append(full(arr))
            args.append(arr)
        in_specs.append(full(wo))
        args.append(wo)
        reach = (HALO + CONV_WIDTH // 2) // SUBLANE * SUBLANE
        n_sub = TM // p_seq
        scratch = [pltpu.VMEM((n_sub, p_seq + 2 * HALO, bw), F32),
                   pltpu.VMEM((n_sub, SUBLANE - 1, p_seq + reach, bw), F32)]
    else:
        in_specs.append(full(wo))
        args.append(wo)
    in_specs += _x_specs(xs, pt, d)
    args += list(xs)
    in_specs += [pl.BlockSpec((1, 1, N_MOD, d),
                              lambda t: (layer, jnp.where(t < pt, 0, 1 + (t - pt) // st), 0, 0)),
                 full(g2), full(rw2)]
    args += [mod, g2, rw2]
    return pl.pallas_call(
        functools.partial(_post_kernel, even=even, n_x=len(xs), pt=pt, st=st),
        grid=(nt,),
        in_specs=in_specs,
        out_specs=[tok(d), tok(d), pl.BlockSpec((1, N_EXPERTS, TM), lambda t: (t, 0, 0))],
        out_shape=[jax.ShapeDtypeStruct((t_tok, d), F32), jax.ShapeDtypeStruct((t_tok, d), BF16),
                   jax.ShapeDtypeStruct((nt, N_EXPERTS, TM), F32)],
        scratch_shapes=scratch,
        compiler_params=pltpu.CompilerParams(dimension_semantics=("arbitrary",),
                                             vmem_limit_bytes=VMEM_LIMIT),
        name="post_even" if even else "post_odd",
    )(*args)


def _stage_rows():
    return 2 * TD + N_EXPERTS * BF16_ROWS


def _chunk_bits():
    bits = []
    b = TD
    while b >= BF16_ROWS:
        bits.append(b)
        b //= 2
    return bits


def _moe_plan(cnt, t_tok):
    cnt16 =(cnt + BF16_ROWS - 1) // BF16_ROWS * BF16_ROWS
    loff = jnp.cumsum(cnt16, axis=1) - cnt16
    tot = cnt16.sum(axis=0)
    reg = (tot + TF - 1) // TF * TF
    off = jnp.cumsum(reg) - reg
    goff = off[None, :] + jnp.cumsum(cnt16, axis=0) - cnt16
    ends = jnp.cumsum(reg // TF)
    n_used = ends[-1]
    n_tiles = _sorted_rows(t_tok) // TF
    tile = jnp.minimum(jnp.arange(n_tiles, dtype=I32), n_used - 1)
    tile_expert = jnp.minimum(jnp.sum(tile[:, None] >= ends[None, :], axis=1), N_EXPERTS - 1).astype(I32)
    tail = jnp.stack([n_used * TF, n_tiles - n_used])
    return dict(cnt16=cnt16.reshape(-1), loff=loff.reshape(-1), goff=goff.reshape(-1),
                gap_start=jnp.concatenate([off + tot, tail[0:1]]).astype(I32),
                gap_len=jnp.concatenate([reg - tot, tail[1:2]]).astype(I32),
                tile=tile.astype(I32), tile_expert=tile_expert,
                n_used=jnp.reshape(n_used, (1,)).astype(I32))


def _sorted_rows(t_tok):
    nd = t_tok // TD
    worst = 2 * t_tok + nd * N_EXPERTS * (BF16_ROWS - 1) + N_EXPERTS * (TF - 1)
    return (worst + TF - 1) // TF * TF


def _perm(route, loff_ref, tile):
    e0 = route[0:1].astype(I32)
    e1 = route[1:2].astype(I32)
    eidx = lax.broadcasted_iota(I32, (N_EXPERTS, TD), 0)
    m0 = eidx == e0
    m1 = eidx == e1
    chosen = jnp.where(m0, 1.0, jnp.where(m1, 1.0, 0.0)).astype(BF16)
    before = (lax.broadcasted_iota(I32, (TD, TD), 0) < lax.broadcasted_iota(I32, (TD, TD), 1))
    rank = _mm(chosen, jnp.where(before, 1.0, 0.0).astype(BF16))
    ecol = lax.broadcasted_iota(I32, (N_EXPERTS, 1), 0)
    lcol = jnp.zeros((N_EXPERTS, 1), F32)
    for e in range(N_EXPERTS):
        lcol = jnp.where(ecol == e, loff_ref[tile * N_EXPERTS + e].astype(F32), lcol)
    base = rank + lcol
    slot0 = jnp.sum(jnp.where(m0, base, 0.0), axis=0, keepdims=True).astype(I32)
    slot1 = jnp.sum(jnp.where(m1, base, 0.0), axis=0, keepdims=True).astype(I32)
    rio = lax.broadcasted_iota(I32, (_stage_rows(), TD), 0)
    return rio == slot0, rio == slot1


def _chunk_copies(cnt_ref, loff_ref, goff_ref, tile, stage, sorted_hbm, sem, *, to_sorted, wait):
    if wait:
        total = cnt_ref[tile * N_EXPERTS]
        for e in range(1, N_EXPERTS):
            total = total + cnt_ref[tile * N_EXPERTS + e]
        bit = 2 * TD
        while bit >= BF16_ROWS:
            s_view = stage.at[pl.ds(0, bit), :]
            h_view = sorted_hbm.at[pl.ds(0, bit), :]
            cp = (pltpu.make_async_copy(s_view, h_view, sem) if to_sorted
                  else pltpu.make_async_copy(h_view, s_view, sem))

            @pl.when((total & bit) != 0)
            def _(cp=cp):
                cp.wait()

            bit //= 2
        return
    for e in range(N_EXPERTS):
        n = cnt_ref[tile * N_EXPERTS + e]
        lo = loff_ref[tile * N_EXPERTS + e]
        go = goff_ref[tile * N_EXPERTS + e]
        for bit in _chunk_bits():
            done = n & ~(2 * bit - 1)
            s_view = stage.at[pl.ds(pl.multiple_of(lo + done, BF16_ROWS), bit), :]
            h_view = sorted_hbm.at[pl.ds(pl.multiple_of(go + done, BF16_ROWS), bit), :]
            cp = (pltpu.make_async_copy(s_view, h_view, sem) if to_sorted
                  else pltpu.make_async_copy(h_view, s_view, sem))

            @pl.when((n & bit) != 0)
            def _(cp=cp):
                cp.wait() if wait else cp.start()


def _dispatch_kernel(cnt_ref, loff_ref, goff_ref, gs_ref, gl_ref,
                     route_ref, h2_ref, xs_hbm, stage, zeros, sem, zsem):
    i = pl.program_id(0)
    n = pl.num_programs(0)
    slot = i % 2
    p0, p1 = _perm(route_ref[...], loff_ref, i)
    p = jnp.where(p0, 1.0, jnp.where(p1, 1.0, 0.0)).astype(BF16)
    stage[slot] =_mm(p, h2_ref[...]).astype(BF16)
    copies = functools.partial(_chunk_copies, cnt_ref, loff_ref, goff_ref, to_sorted=True)
    copies(i, stage.at[slot], xs_hbm, sem.at[slot], wait=False)

    @pl.when(i > 0)
    def _():
        copies(i - 1, stage.at[1 - slot], xs_hbm, sem.at[1 - slot], wait=True)

    @pl.when(i == n - 1)
    def _():
        zeros[...] = jnp.zeros(zeros.shape, zeros.dtype)
        bits = [b for b in _chunk_bits() if b < TF]
        for wait in (False, True):
            for e in range(N_EXPERTS):
                g0, gl = gs_ref[e], gl_ref[e]
                for bit in bits:
                    done = gl & ~(2 * bit - 1)
                    cp = pltpu.make_async_copy(
                        zeros.at[pl.ds(0, bit), :],
                        xs_hbm.at[pl.ds(pl.multiple_of(g0 + done, BF16_ROWS), bit), :], zsem)

                    @pl.when((gl & bit) != 0)
                    def _(cp=cp, wait=wait):
                        cp.wait() if wait else cp.start()
        tail0, ntail = gs_ref[N_EXPERTS], gl_ref[N_EXPERTS]

        def tail_copy(j):
            return pltpu.make_async_copy(
                zeros, xs_hbm.at[pl.ds(pl.multiple_of(tail0 + j * TF, TF), TF), :], zsem)

        @pl.loop(0, ntail)
        def _(j):
            tail_copy(j).start()

        @pl.loop(0, ntail)
        def _(j):
            tail_copy(j).wait()

        copies(i, stage.at[slot], xs_hbm, sem.at[slot], wait=True)


def _dispatch(plan, route, h2):
    t_tok, d = h2.shape
    nd = t_tok // TD
    rows = _sorted_rows(t_tok)
    grid_spec = pltpu.PrefetchScalarGridSpec(
        num_scalar_prefetch=5,
        grid=(nd,),
        in_specs=[pl.BlockSpec((8, TD), lambda i, *_: (0, i)),
                  pl.BlockSpec((TD, d), lambda i, *_: (i, 0))],
        out_specs=pl.BlockSpec(memory_space=pl.ANY),
        scratch_shapes=[pltpu.VMEM((2, _stage_rows(), d), BF16), pltpu.VMEM((TF, d), BF16),
                        pltpu.SemaphoreType.DMA((2,)), pltpu.SemaphoreType.DMA(())],
    )
    return pl.pallas_call(
        _dispatch_kernel,
        grid_spec=grid_spec,
        out_shape=jax.ShapeDtypeStruct((rows, d), BF16),
        compiler_params=pltpu.CompilerParams(dimension_semantics=("arbitrary",),
                                             vmem_limit_bytes=VMEM_LIMIT),
        name="moe_dispatch",
    )(plan["cnt16"], plan["loff"], plan["goff"], plan["gap_start"], plan["gap_len"], route, h2)


def _ffn_kernel(tile_ref, te_ref, nu_ref, x_ref, w1_ref, w3_ref, w2_ref, y_ref, w1b, w3b, w2b):
    i = pl.program_id(0)

    @pl.when(jnp.logical_or(i == 0, te_ref[i] != te_ref[jnp.maximum(i - 1, 0)]))
    def _():
        w1b[...] = w1_ref[0, 0].astype(BF16)
        w3b[...] = w3_ref[0, 0].astype(BF16)
        w2b[...] = w2_ref[0, 0].astype(BF16)

    @pl.when(i < nu_ref[0])
    def _():
        x = x_ref[...]
        a = _mm(x, w1b[...])
        b = _mm(x, w3b[...])
        he = (a * _sigmoid(a) * b).astype(BF16)
        y_ref[...] = _mm(he, w2b[...]).astype(y_ref.dtype)

    @pl.when(i >= nu_ref[0])
    def _():
        y_ref[...] = jnp.zeros(y_ref.shape, y_ref.dtype)


def _ffn(plan, xs, w1, w3, w2, layer):
    rows, d = xs.shape
    f = w1.shape[3]
    grid_spec = pltpu.PrefetchScalarGridSpec(
        num_scalar_prefetch=3,
        grid=(rows // TF,),
        in_specs=[pl.BlockSpec((TF, d), lambda i, tile, te, nu: (tile[i], 0)),
                  pl.BlockSpec((1, 1, d, f), lambda i, tile, te, nu: (layer, te[i], 0, 0)),
                  pl.BlockSpec((1, 1, d, f), lambda i, tile, te, nu: (layer, te[i], 0, 0)),
                  pl.BlockSpec((1, 1, f, d), lambda i, tile, te, nu: (layer, te[i], 0, 0))],
        out_specs=pl.BlockSpec((TF, d), lambda i, tile, te, nu: (i, 0)),
        scratch_shapes=[pltpu.VMEM((d, f), BF16), pltpu.VMEM((d, f), BF16), pltpu.VMEM((f, d), BF16)],
    )
    return pl.pallas_call(
        _ffn_kernel,
        grid_spec=grid_spec,
        out_shape=jax.ShapeDtypeStruct((rows, d), BF16),
        compiler_params=pltpu.CompilerParams(dimension_semantics=("arbitrary",),
                                             vmem_limit_bytes=VMEM_LIMIT),
        name="moe_ffn",
    )(plan["tile"], plan["tile_expert"], plan["n_used"], xs, w1, w3, w2)


def _combine_kernel(*refs, final, ptd):
    cnt_ref, loff_ref, goff_ref, route_ref, ys_hbm, xn_ref, mod_ref = refs[:7]
    if final:
        fg_ref, yp_ref, ysm_ref, stage, sem = refs[7:]
    else:
        xo_ref, stage, sem = refs[7:]
    i = pl.program_id(0)
    n = pl.num_programs(0)
    slot = i % 2
    copies = functools.partial(_chunk_copies, cnt_ref, loff_ref, goff_ref, to_sorted=False)

    @pl.when(i == 0)
    def _():
        stage[...] = jnp.zeros(stage.shape, stage.dtype)
        copies(i, stage.at[0], ys_hbm, sem.at[0], wait=False)

    @pl.when(i + 1 < n)
    def _():
        copies(i + 1, stage.at[1 - slot], ys_hbm, sem.at[1 - slot], wait=False)

    route = route_ref[...]
    p0, p1 = _perm(route, loff_ref, i)
    gate_rows = jnp.where(p0, route[2:3], jnp.where(p1, route[3:4], 0.0))
    gate_col = jnp.sum(gate_rows, axis=1, keepdims=True)
    p = jnp.where(p0, 1.0, jnp.where(p1, 1.0, 0.0)).astype(BF16)
    copies(i, stage.at[slot], ys_hbm, sem.at[slot], wait=True)
    scaled = (stage[slot].astype(F32) * gate_col).astype(BF16)
    y = lax.dot_general(p, scaled, (((0,), (0,)), ((), ())), preferred_element_type=F32)
    xo = xn_ref[...] + mod_ref[0, 0][5:6] * y
    if final:
        out = _rms(xo, fg_ref[...])

        @pl.when(i < ptd)
        def _():
            yp_ref[...] = out

        @pl.when(i >= ptd)
        def _():
            ysm_ref[...] = out
    else:
        xo_ref[...] = xo


def _combine(plan, route, ys, xn, mod, final_g, *, layer, final, p_tok, s_seq):
    t_tok, d = xn.shape
    nd = t_tok // TD
    ptd = p_tok // TD
    std = s_seq // TD
    in_specs = [pl.BlockSpec((8, TD), lambda i, *_: (0, i)),
                pl.BlockSpec(memory_space=pl.ANY),
                pl.BlockSpec((TD, d), lambda i, *_: (i, 0)),
                pl.BlockSpec((1, 1, N_MOD, d),
                             lambda i, *_: (layer, jnp.where(i < ptd, 0, 1 + (i - ptd) // std), 0, 0))]
    args = [route, ys, xn, mod]
    if final:
        in_specs.append(pl.BlockSpec((1, d), lambda i, *_: (0, 0)))
        args.append(final_g)
        out_specs = [pl.BlockSpec((TD, d), lambda i, *_: (jnp.minimum(i, ptd - 1), 0)),
                     pl.BlockSpec((TD, d), lambda i, *_: (jnp.maximum(i - ptd, 0), 0))]
        out_shape = [jax.ShapeDtypeStruct((p_tok, d), F32),
                     jax.ShapeDtypeStruct((t_tok - p_tok, d), F32)]
    else:
        out_specs = pl.BlockSpec((TD, d), lambda i, *_: (i, 0))
        out_shape = jax.ShapeDtypeStruct((t_tok, d), F32)
    grid_spec = pltpu.PrefetchScalarGridSpec(
        num_scalar_prefetch=3, grid=(nd,), in_specs=in_specs, out_specs=out_specs,
        scratch_shapes=[pltpu.VMEM((2, _stage_rows(), d), BF16), pltpu.SemaphoreType.DMA((2,))])
    return pl.pallas_call(
        functools.partial(_combine_kernel, final=final, ptd=ptd),
        grid_spec=grid_spec, out_shape=out_shape,
        compiler_params=pltpu.CompilerParams(dimension_semantics=("arbitrary",),
                                             vmem_limit_bytes=VMEM_LIMIT),
        name="moe_combine_final" if final else "moe_combine",
    )(plan["cnt16"], plan["loff"], plan["goff"], *args)


def kernel(x_prompt, x_sample, cache_even_k, cache_even_v, cache_odd_k, cache_odd_v, c, c_ctx, w_ada, b_ada, norm_g, final_g, even_w_in, even_w_out, even_lam, even_subln_g, even_conv_w, even_conv_b, even_conv_norm_g, even_conv_norm_b, odd_w_qkv, odd_w_out, odd_rpb, router_w, router_b, moe_w1, moe_w3, moe_w2):
    batch, seq, d = x_prompt.shape
    dbatch, dseq, _ = x_sample.shape
    past = cache_even_k.shape[2]
    depth = w_ada.shape[0]
    p_tok, s_tok = batch * seq, dbatch * dseq
    t_tok = p_tok + s_tok
    assert TM % seq == 0 and seq % CONV_ROWS == 0 and seq >= HALO
    assert dseq % TM == 0 and dseq % TD == 0 and p_tok % TM == 0 and p_tok % TD == 0 and p_tok % dseq == 0
    assert 1 + dbatch <= COND_ROWS and dseq % GRID_W == 0 and past % TD == 0
    assert dseq // GRID_W >= NA_KROWS + NA_QROWS

    cond = jnp.zeros((COND_ROWS, d), F32).at[0].set(c_ctx).at[1:1 + dbatch].set(c)
    mod = _ada(cond, w_ada, b_ada).reshape(depth, COND_ROWS, N_MOD, d)
    rope = _rope_tables(dseq)
    rwh, rwl = _split_bf16(jnp.pad(router_w, ((0, 0), (0, LANE - N_EXPERTS))))
    rw2 = jnp.concatenate([rwh, rwl], axis=1)
    rbias = router_b.reshape(N_EXPERTS, 1)
    xs = [x_prompt.reshape(p_tok, d), x_sample.reshape(s_tok, d)]
    outs = {}

    for layer in range(depth):
        j = layer // 2
        even = layer % 2 == 0
        g1 = norm_g[layer, 0].reshape(1, d)
        g2 = norm_g[layer, 1].reshape(1, d)
        if even:
            q, k, v, u, kf, vf = _pre(xs, mod, g1, even_w_in[j].astype(BF16), rope, layer=layer,
                                      even=True, p_tok=p_tok, p_seq=seq, s_seq=dseq,
                                      heads=(A_HEADS, A_V_DIM))
            outs["even_k"], outs["even_v"] = kf, vf
            lam_init = 0.8 - 0.6 * float(np.exp(-0.3 * layer))
            lam = even_lam[j]
            subln = even_subln_g[j].reshape(1, A_V_DIM)
            ap = _pair_attn(q, k, v, batch=batch, n_pairs=A_HEADS, nq_tok=seq, nk_tok=seq, q_row0=0,
                            k_row0=0, tq=seq, tk=seq, groups=PROMPT_GROUPS, diff=True, lam=lam,
                            subln=subln, lam_init=lam_init)
            ctx = (cache_even_k.reshape(dbatch, -1, past, A_WIDTH),
                   cache_even_v.reshape(dbatch, -1, past, A_WIDTH), j)
            a_s = _pair_attn(q, k, v, batch=dbatch, n_pairs=A_HEADS, nq_tok=dseq, nk_tok=dseq,
                             q_row0=p_tok, k_row0=p_tok, tq=ATT_TQ, tk=ATT_TK, groups=ATT_GROUPS, ctx=ctx, diff=True,
                             lam=lam, subln=subln, lam_init=lam_init)
            conv = [even_conv_w[j], even_conv_b[j].reshape(1, -1), even_conv_norm_g[j].reshape(1, -1),
                    even_conv_norm_b[j].reshape(1, -1)]
            xn, h2, logits = _post((ap, a_s), u, conv, even_w_out[j].astype(BF16), xs, mod, g2, rw2,
                                   layer=layer, even=True, p_tok=p_tok, p_seq=seq, s_seq=dseq)
        else:
            q, k, v, kf, vf = _pre(xs, mod, g1, odd_w_qkv[j].astype(BF16), None, layer=layer,
                                   even=False, p_tok=p_tok, p_seq=seq, s_seq=dseq,
                                   heads=(C_HEADS, C_HEAD_DIM))
            outs["odd_k"] = kf.reshape(batch, 1, seq, C_HEADS, C_HEAD_DIM)
            outs["odd_v"] = vf.reshape(batch, 1, seq, C_HEADS, C_HEAD_DIM)
            n_pairs = d // LANE
            op = _pair_attn(q, k, v, batch=batch, n_pairs=n_pairs, nq_tok=seq, nk_tok=seq, q_row0=0,
                            k_row0=0, tq=seq, tk=seq, groups=PROMPT_GROUPS)
            bias = _na_bias(odd_rpb[j])
            o_s = _na(q, k, v, cache_odd_k.reshape(dbatch, -1, past, d),
                      cache_odd_v.reshape(dbatch, -1, past, d), j, bias,
                      batch=dbatch, n_tok=dseq, row0=p_tok)
            xn, h2, logits = _post((op, o_s), None, None, odd_w_out[j].astype(BF16), xs, mod, g2, rw2,
                                   layer=layer, even=False, p_tok=p_tok, p_seq=seq, s_seq=dseq)
        route, cnt = _route(logits, rbias)
        plan = _moe_plan(cnt, t_tok)
        x_sorted = _dispatch(plan, route, h2)
        y_sorted = _ffn(plan, x_sorted, moe_w1, moe_w3, moe_w2, layer)
        final = layer == depth - 1
        res = _combine(plan, route, y_sorted, xn, mod, final_g.reshape(1, d), layer=layer, final=final,
                       p_tok=p_tok, s_seq=dseq)
        if final:
            y_prompt, y_sample = res
        else:
            xs = [res]

    return (y_prompt.reshape(batch, seq, d), y_sample.reshape(dbatch, dseq, d),
            outs["even_k"], outs["even_v"], outs["odd_k"], outs["odd_v"])
```
